```python
import jax, jax.numpy as jnp
from jax import lax
import numpy as np

D_MODEL = 1024
BATCH = 2
SEQ = 8192
DEPTH = 2
DEC_BATCH = 128
DEC_SEQ = 4
PAST_LEN = 16384
PAGE_SIZE = 128

PLE_DIM = 256
POOL_WIDTH = D_MODEL
POOL_WINDOWS = (2, 4, 8, 16)
N_POOL_GROUPS = len(POOL_WINDOWS)
POOL_GROUP = POOL_WIDTH // N_POOL_GROUPS
POOL_STATE = max(POOL_WINDOWS) - 1
HEAD_DIM = 64
N_HEADS = D_MODEL // HEAD_DIM
N_KV_HEADS = 4
GQA_GROUP = N_HEADS // N_KV_HEADS
ATTN_WIDTH = N_HEADS * HEAD_DIM
KV_WIDTH = N_KV_HEADS * HEAD_DIM
WINDOW = 128
BLOCK = 128
ROPE_THETA = 500000.0
ROPE_DIM = HEAD_DIM // 4
EPS = 1e-6
IN_SPLIT_SIZES = (POOL_WIDTH, POOL_WIDTH, ATTN_WIDTH, KV_WIDTH, KV_WIDTH, ATTN_WIDTH, D_MODEL, D_MODEL)
IN_COLS = sum(IN_SPLIT_SIZES)
IN_SPLIT_POINTS = tuple(int(c) for c in np.cumsum(IN_SPLIT_SIZES)[:-1])

kernel_name = "hybrid_pool_swa_gated_decoder_step"


def rms_norm(x, g):
    xf = x.astype(jnp.float32)
    y = xf * lax.rsqrt(jnp.mean(xf * xf, axis=-1, keepdims=True) + EPS) * g.astype(jnp.float32)
    return y.astype(x.dtype)


def rope(x, pos):
    inv_freq = ROPE_THETA ** (-jnp.arange(0, ROPE_DIM, 2, dtype=jnp.float32) / ROPE_DIM)
    ang = pos.astype(jnp.float32)[:, None] * inv_freq[None, :]
    cos = jnp.cos(ang)[None, :, None, :]
    sin = jnp.sin(ang)[None, :, None, :]
    xr = x[..., :ROPE_DIM].astype(jnp.float32)
    x1, x2 = xr[..., :ROPE_DIM // 2], xr[..., ROPE_DIM // 2:]
    rot = jnp.concatenate([x1 * cos - x2 * sin, x2 * cos + x1 * sin], axis=-1).astype(x.dtype)
    return jnp.concatenate([rot, x[..., ROPE_DIM:]], axis=-1)


def multiscale_pool(u_pad, pos, pool_map, pool_scale):
    B, L, P = u_pad.shape
    T = L - POOL_STATE
    uf = u_pad.astype(jnp.float32)
    cs = jnp.concatenate([jnp.zeros_like(uf[:, :1]), jnp.cumsum(uf, axis=1)], axis=1)
    end = cs[:, POOL_STATE + 1:]
    u = uf[:, POOL_STATE:]
    outs = []
    for g, w in enumerate(POOL_WINDOWS):
        sl = slice(g * POOL_GROUP, (g + 1) * POOL_GROUP)
        start = cs[:, POOL_STATE + 1 - w: POOL_STATE + 1 - w + T, sl]
        cnt = jnp.minimum(pos + 1, w).astype(jnp.float32)[None, :, None]
        outs.append((end[..., sl] - start) / cnt - u[..., sl])
    r = jnp.stack(outs, axis=2)
    r = jnp.einsum('btgc,gcd->btgd', r, pool_map.astype(jnp.float32)).reshape(B, T, P)
    return (r * pool_scale.astype(jnp.float32)).astype(u_pad.dtype)


def sw_attention_core(q, k, v, qpos, kpos, sinks):
    s = jnp.einsum('bnqkgd,bnskd->bnkgqs', q.astype(jnp.float32), k.astype(jnp.float32)) * (HEAD_DIM ** -0.5)
    rel = qpos[:, :, None] - kpos[:, None, :]
    mask = (rel >= 0) & (rel <= WINDOW) & (kpos[:, None, :] >= 0)
    s = jnp.where(mask[None, :, None, None], s, -jnp.inf)
    sink = sinks.astype(jnp.float32).reshape(N_KV_HEADS, GQA_GROUP)[None, None, :, :, None, None]
    m = jnp.maximum(jnp.max(s, axis=-1, keepdims=True), sink)
    e = jnp.exp(s - m)
    pr = e / (jnp.sum(e, axis=-1, keepdims=True) + jnp.exp(sink - m))
    return jnp.einsum('bnkgqs,bnskd->bnqkgd', pr, v.astype(jnp.float32))


def layer(h, p_l, pos, pool_prev, k_prev, v_prev, norm_g, w_in, q_norm_g, k_norm_g, sinks,
          pool_map, pool_scale, w_proj_pool, w_proj_attn, w_out, w_ple, w_ple_gate):
    B, T, _ = h.shape
    xn = rms_norm(h, norm_g)
    z = xn @ w_in
    u, zp, q, k, v, za, ga, gb = jnp.split(z, IN_SPLIT_POINTS, axis=-1)
    if pool_prev is None:
        pool_prev = jnp.zeros((B, POOL_STATE, POOL_WIDTH), u.dtype)
    u_pad = jnp.concatenate([pool_prev, u], axis=1)
    ya = multiscale_pool(u_pad, pos, pool_map, pool_scale) * jax.nn.silu(zp)
    new_pool = u_pad[:, -POOL_STATE:]
    q = rope(rms_norm(q.reshape(B, T, N_HEADS, HEAD_DIM), q_norm_g), pos)
    k = rope(rms_norm(k.reshape(B, T, N_KV_HEADS, HEAD_DIM), k_norm_g), pos)
    v = v.reshape(B, T, N_KV_HEADS, HEAD_DIM)
    if k_prev is None:
        nb = T // BLOCK
        qb = q.reshape(B, nb, BLOCK, N_KV_HEADS, GQA_GROUP, HEAD_DIM)
        kb = k.reshape(B, nb, BLOCK, N_KV_HEADS, HEAD_DIM)
        vb = v.reshape(B, nb, BLOCK, N_KV_HEADS, HEAD_DIM)
        kcat = jnp.concatenate([jnp.concatenate([jnp.zeros_like(kb[:, :1]), kb[:, :-1]], axis=1), kb], axis=2)
        vcat = jnp.concatenate([jnp.concatenate([jnp.zeros_like(vb[:, :1]), vb[:, :-1]], axis=1), vb], axis=2)
        qpos = pos.reshape(nb, BLOCK)
        kpos = jnp.concatenate([qpos - BLOCK, qpos], axis=1)
        o = sw_attention_core(qb, kcat, vcat, qpos, kpos, sinks)
        new_k, new_v = k[:, -WINDOW:], v[:, -WINDOW:]
    else:
        kc = jnp.concatenate([k_prev, k], axis=1)
        vc = jnp.concatenate([v_prev, v], axis=1)
        kpos = jnp.concatenate([pos[0] - WINDOW + jnp.arange(WINDOW, dtype=jnp.int32), pos])[None]
        qb = q.reshape(B, 1, T, N_KV_HEADS, GQA_GROUP, HEAD_DIM)
        o = sw_attention_core(qb, kc[:, None], vc[:, None], pos[None], kpos, sinks)
        new_k, new_v = kc[:, -WINDOW:], vc[:, -WINDOW:]
    yb = o.reshape(B, T, ATTN_WIDTH).astype(h.dtype) * jax.nn.silu(za)
    m = jax.nn.sigmoid(ga) * (ya @ w_proj_pool) + jax.nn.sigmoid(gb) * (yb @ w_proj_attn)
    h = h + m @ w_out
    h = h + jax.nn.sigmoid(h @ w_ple_gate) * (p_l @ w_ple)
    return h, new_pool, new_k, new_v


def setup_inputs(seed: int = 0) -> dict:
    key = jax.random.key(seed)
    ks = jax.random.split(key, 20)
    f32 = jnp.float32
    nrm = lambda k, s, sc: jax.random.normal(k, s, f32) * sc
    return {
        "x_prompt": nrm(ks[0], (BATCH, SEQ, D_MODEL), 1.0),
        "x_sample": nrm(ks[1], (DEC_BATCH, DEC_SEQ, D_MODEL), 1.0),
        "cache_k": nrm(ks[2], (DEPTH, DEC_BATCH, WINDOW, N_KV_HEADS, HEAD_DIM), 1.0),
        "cache_v": nrm(ks[3], (DEPTH, DEC_BATCH, WINDOW, N_KV_HEADS, HEAD_DIM), 1.0),
        "state_pool": nrm(ks[4], (DEPTH, DEC_BATCH, POOL_STATE, POOL_WIDTH), 1.0),
        "p_prompt": nrm(ks[5], (DEPTH, BATCH, SEQ, PLE_DIM), 1.0),
        "p_sample": nrm(ks[6], (DEPTH, DEC_BATCH, DEC_SEQ, PLE_DIM), 1.0),
        "norm_g": 1.0 + nrm(ks[7], (DEPTH, D_MODEL), 0.05),
        "w_in": nrm(ks[8], (DEPTH, D_MODEL, IN_COLS), D_MODEL ** -0.5),
        "q_norm_g": 1.0 + nrm(ks[9], (DEPTH, HEAD_DIM), 0.05),
        "k_norm_g": 1.0 + nrm(ks[10], (DEPTH, HEAD_DIM), 0.05),
        "sinks": nrm(ks[11], (DEPTH, N_HEADS), 0.5),
        "pool_map": nrm(ks[12], (DEPTH, N_POOL_GROUPS, POOL_GROUP, POOL_GROUP), POOL_GROUP ** -0.5),
        "pool_scale": 1.0 + nrm(ks[13], (DEPTH, POOL_WIDTH), 0.1),
        "w_proj_pool": nrm(ks[14], (DEPTH, POOL_WIDTH, D_MODEL), POOL_WIDTH ** -0.5),
        "w_proj_attn": nrm(ks[15], (DEPTH, ATTN_WIDTH, D_MODEL), ATTN_WIDTH ** -0.5),
        "w_out": nrm(ks[16], (DEPTH, D_MODEL, D_MODEL), D_MODEL ** -0.5),
        "w_ple": nrm(ks[17], (DEPTH, PLE_DIM, D_MODEL), PLE_DIM ** -0.5),
        "w_ple_gate": nrm(ks[18], (DEPTH, D_MODEL, D_MODEL), D_MODEL ** -0.5),
    }


def reference(x_prompt, x_sample, cache_k, cache_v, state_pool, p_prompt, p_sample,
              norm_g, w_in, q_norm_g, k_norm_g, sinks, pool_map, pool_scale,
              w_proj_pool, w_proj_attn, w_out, w_ple, w_ple_gate):
    pos_prompt = jnp.arange(SEQ, dtype=jnp.int32)
    pos_sample = PAST_LEN + jnp.arange(DEC_SEQ, dtype=jnp.int32)
    hp, hs = x_prompt, x_sample
    kp_l, vp_l, pp_l, ks_l, vs_l, ps_l = [], [], [], [], [], []
    for i in range(DEPTH):
        w = (norm_g[i], w_in[i], q_norm_g[i], k_norm_g[i], sinks[i], pool_map[i], pool_scale[i],
             w_proj_pool[i], w_proj_attn[i], w_out[i], w_ple[i], w_ple_gate[i])
        hp, npool, nk, nv = layer(hp, p_prompt[i], pos_prompt, None, None, None, *w)
        pp_l.append(npool); kp_l.append(nk); vp_l.append(nv)
        hs, npool, nk, nv = layer(hs, p_sample[i], pos_sample, state_pool[i], cache_k[i], cache_v[i], *w)
        ps_l.append(npool); ks_l.append(nk); vs_l.append(nv)
    new_k_prompt = jnp.stack(kp_l, axis=0)
    new_v_prompt = jnp.stack(vp_l, axis=0)
    new_pool_prompt = jnp.stack(pp_l, axis=0)
    new_k_sample = jnp.stack(ks_l, axis=0)
    new_v_sample = jnp.stack(vs_l, axis=0)
    new_pool_sample = jnp.stack(ps_l, axis=0)
    return (hp, hs, new_k_prompt, new_v_prompt, new_pool_prompt, new_k_sample, new_v_sample, new_pool_sample)
```

```python
import functools
import math

import numpy as np
import jax
import jax.numpy as jnp
from jax import lax
from jax.experimental import pallas as pl
from jax.experimental.pallas import tpu as pltpu

D_MODEL = 1024
PLE_DIM = 256
POOL_WINDOWS = (2, 4, 8, 16)
POOL_GROUP = D_MODEL // len(POOL_WINDOWS)
POOL_STATE = max(POOL_WINDOWS) - 1
HEAD_DIM = 64
N_HEADS = D_MODEL // HEAD_DIM
N_KV_HEADS = 4
GQA_GROUP = N_HEADS // N_KV_HEADS
KV_WIDTH = N_KV_HEADS * HEAD_DIM
WINDOW = 128
ROPE_THETA = 500000.0
ROPE_DIM = HEAD_DIM // 4
EPS = 1e-6
PAST_LEN = 16384

C_U, C_ZP, C_Q, C_K, C_V, C_ZA, C_GA, C_GB, C_END = 0, 1024, 2048, 3072, 3328, 3584, 4608, 5632, 6656

LANES = 128
SUBLANES = 8
NEG = -1e30
LOG2E = math.log2(math.e)
VMEM_LIMIT = 60 * 1024 * 1024

TQ = 256
SB = 8

F32 = jnp.float32
BF16 = jnp.bfloat16


def _sigmoid(x):
    return 1.0 / (1.0 + jnp.exp2(x * (-LOG2E)))


def _dot(a, b):
    return jnp.dot(a, b, preferred_element_type=F32)


def _dot_t(a, b):
    return lax.dot_general(a, b, (((1,), (1,)), ((), ())), preferred_element_type=F32)


def _rope(x, cos, s1, s2):
    outs = []
    for c in range(x.shape[1] // LANES):
        xc = x[:, c * LANES:(c + 1) * LANES]
        outs.append(xc * cos + pltpu.roll(xc, LANES - ROPE_DIM // 2, 1) * s1 + pltpu.roll(xc, ROPE_DIM // 2, 1) * s2)
    return outs[0] if len(outs) == 1 else jnp.concatenate(outs, axis=1)


def _head_norm(x, hn, gain):
    sq = (x * x).astype(BF16)
    ms = jnp.concatenate([_dot(sq[:, c * 256:(c + 1) * 256], hn) for c in range(x.shape[1] // 256)], axis=1)
    return x * lax.rsqrt(ms + EPS) * gain


def _rms_in(h, g):
    ms = jnp.mean(h * h, axis=-1, keepdims=True)
    return (h * lax.rsqrt(ms + EPS) * g).astype(BF16)


def _tail(h, p, ya, yb, ga, gb, wpp, wpa, wout, wg, wple):
    m = _sigmoid(ga) * _dot(ya.astype(BF16), wpp) + _sigmoid(gb) * _dot(yb.astype(BF16), wpa)
    h1 = h + _dot(m.astype(BF16), wout)
    gate = _sigmoid(_dot(h1.astype(BF16), wg))
    return h1 + gate * _dot(p.astype(BF16), wple)


def _pool_map(r, pm_ref, ps):
    rb = r.astype(BF16)
    mapped = jnp.concatenate([_dot(rb[:, g * POOL_GROUP:(g + 1) * POOL_GROUP], pm_ref[g])
                              for g in range(len(POOL_WINDOWS))], axis=1)
    return mapped * ps


def _prompt_kernel(sinks_ref, h_ref, p_ref, cos_ref, s1_ref, s2_ref, bias_ref,
                   ng_ref, gq_ref, gk_ref, ps_ref, win_ref, hn_ref, pm_ref,
                   wpp_ref, wpa_ref, wout_ref, wg_ref, wple_ref,
                   y_ref, nk_ref, nv_ref, npool_ref,
                   kall, vall, u_s, s2_s, s4_s, s8_s, o_s):
    t = pl.program_id(1)
    nqb = TQ // WINDOW
    lane = lax.broadcasted_iota(jnp.int32, (WINDOW, LANES), 1)
    lo = lane < HEAD_DIM

    @pl.when(t == 0)
    def _():
        kall[...] = jnp.zeros_like(kall)
        vall[:, :, 0:LANES] = jnp.zeros((N_KV_HEADS, (nqb + 1) * 256, LANES), BF16)
        ones_pat = jnp.concatenate([jnp.where(lo, 1.0, 0.0), jnp.where(lo, 0.0, 1.0)], axis=0).astype(BF16)
        for kh in range(N_KV_HEADS):
            for x in range(nqb + 1):
                vall[kh, x * 256:(x + 1) * 256, LANES:2 * LANES] = ones_pat
        u_s[0:16, :] = jnp.zeros((16, D_MODEL), F32)
        s2_s[0:16, :] = jnp.zeros((16, D_MODEL), F32)
        s4_s[0:16, :] = jnp.zeros((16, 768), F32)
        s8_s[0:16, :] = jnp.zeros((16, 512), F32)

    @pl.when(t > 0)
    def _():
        for kh in range(N_KV_HEADS):
            kall[kh, 0:256, :] = kall[kh, nqb * 256:(nqb + 1) * 256, :]
            vall[kh, 0:256, 0:LANES] = vall[kh, nqb * 256:(nqb + 1) * 256, 0:LANES]

    h = h_ref[0]
    xn = _rms_in(h, ng_ref[...])

    def proj(c0, c1):
        return _dot(xn, win_ref[:, c0:c1])

    u = proj(C_U, C_ZP)
    u_s[16:16 + TQ, :] = u
    s2 = u + u_s[15:15 + TQ, :]
    s2_s[16:16 + TQ, :] = s2
    s4 = s2[:, 256:] + s2_s[14:14 + TQ, 256:]
    s4_s[16:16 + TQ, :] = s4
    s8 = s4[:, 256:] + s4_s[12:12 + TQ, 256:]
    s8_s[16:16 + TQ, :] = s8
    s16 = s8[:, 256:] + s8_s[8:8 + TQ, 256:]
    pos1 = (t * TQ + 1 + lax.broadcasted_iota(jnp.int32, (TQ, 1), 0)).astype(F32)
    wins = (s2[:, :256], s4[:, :256], s8[:, :256], s16)
    r = jnp.concatenate([wins[g] * (1.0 / jnp.minimum(pos1, float(w))) for g, w in enumerate(POOL_WINDOWS)],
                        axis=1) - u
    npool_ref[0] = u_s[TQ + 1:TQ + 16, :]
    u_s[0:16, :] = u_s[TQ:TQ + 16, :]
    s2_s[0:16, :] = s2_s[TQ:TQ + 16, :]
    s4_s[0:16, :] = s4_s[TQ:TQ + 16, :]
    s8_s[0:16, :] = s8_s[TQ:TQ + 16, :]
    zp = proj(C_ZP, C_Q)
    ya = _pool_map(r, pm_ref, ps_ref[...]) * (zp * _sigmoid(zp))

    cos, s1, s2t = cos_ref[...], s1_ref[...], s2_ref[...]
    hn = hn_ref[...]
    qb = _rope(_head_norm(proj(C_Q, C_K), hn, gq_ref[...]), cos, s1, s2t).astype(BF16)
    kr = _rope(_head_norm(proj(C_K, C_V), hn, gk_ref[...]), cos, s1, s2t)
    v = proj(C_V, C_ZA)
    nk_ref[0] = kr[TQ - WINDOW:, :]
    nv_ref[0] = v[TQ - WINDOW:, :]

    for x in range(nqb):
        rows = slice(x * WINDOW, (x + 1) * WINDOW)
        base = (x + 1) * 256
        for pr in range(N_KV_HEADS // 2):
            for src, dst, col in ((kr, kall, None), (v, vall, slice(0, LANES))):
                a = src[rows, pr * LANES:(pr + 1) * LANES]
                ra = pltpu.roll(a, HEAD_DIM, 1)
                parts = ((jnp.where(lo, a, 0.0), jnp.where(lo, 0.0, ra)),
                         (jnp.where(lo, ra, 0.0), jnp.where(lo, 0.0, a)))
                for e in range(2):
                    kh = 2 * pr + e
                    lh = jnp.concatenate(parts[e], axis=0).astype(BF16)
                    if col is None:
                        dst[kh, base:base + 256, :] = lh
                    else:
                        dst[kh, base:base + 256, col] = lh

    def tiled_bias(b):
        b4 = jnp.concatenate([b[:, :LANES], b[:, :LANES], b[:, LANES:], b[:, LANES:]], axis=1)
        return jnp.concatenate([b4, b4], axis=0)

    biases = [tiled_bias(jnp.where(t == 0, bias_ref[1], bias_ref[0]))] + [tiled_bias(bias_ref[0])] * (nqb - 1)
    for kh in range(N_KV_HEADS):
        for n in range(nqb):
            rows = slice(n * WINDOW, (n + 1) * WINDOW)
            qs = jnp.concatenate([qb[rows, kh * 256:kh * 256 + LANES],
                                  qb[rows, kh * 256 + LANES:(kh + 1) * 256]], axis=0)
            s = _dot_t(qs, kall[kh, n * 256:n * 256 + 512, :]) + biases[n]
            p_rows, sink_rows = [], []
            for rr in range(2):
                rs = slice(rr * WINDOW, (rr + 1) * WINDOW)
                pcols = [None] * 4
                sterm = []
                for e in range(2):
                    sink = sinks_ref[kh * GQA_GROUP + 2 * rr + e]
                    sp = s[rs, e * LANES:(e + 1) * LANES]
                    sc = s[rs, 256 + e * LANES:256 + (e + 1) * LANES]
                    mx = jnp.maximum(jnp.max(jnp.maximum(sp, sc), axis=-1, keepdims=True), sink)
                    pcols[e] = jnp.exp2(sp - mx)
                    pcols[2 + e] = jnp.exp2(sc - mx)
                    sterm.append(jnp.exp2(sink - mx))
                p_rows.append(jnp.concatenate(pcols, axis=1))
                sink_rows.append(jnp.where(lo, sterm[0], sterm[1]))
            pmat = jnp.concatenate(p_rows, axis=0).astype(BF16)
            o2 = _dot(pmat, vall[kh, n * 256:n * 256 + 512, :])
            o = o2[:, :LANES] / (o2[:, LANES:] + jnp.concatenate(sink_rows, axis=0))
            o_s[rows, kh * 256:kh * 256 + LANES] = o[:WINDOW]
            o_s[rows, kh * 256 + LANES:(kh + 1) * 256] = o[WINDOW:]

    za = proj(C_ZA, C_GA)
    yb = o_s[...] * (za * _sigmoid(za))
    y_ref[0] = _tail(h, p_ref[0], ya, yb, proj(C_GA, C_GB), proj(C_GB, C_END),
                     wpp_ref[...], wpa_ref[...], wout_ref[...], wg_ref[...], wple_ref[...])


def _const_spec(shape):
    nd = len(shape)
    return pl.BlockSpec(shape, lambda *_: (0,) * nd, pipeline_mode=pl.Buffered(1))


def _prompt_layer(h, p, tabs, bias, w):
    B, T, _ = h.shape
    nt = T // TQ
    nqb = TQ // WINDOW
    row = lambda width: pl.BlockSpec((1, TQ, width), lambda b, t: (b, t, 0))
    tab = pl.BlockSpec((TQ, LANES), lambda b, t: (t, 0))
    in_specs = [
        pl.BlockSpec(memory_space=pltpu.SMEM),
        row(D_MODEL), row(PLE_DIM), tab, tab, tab,
        _const_spec(bias.shape),
        _const_spec((1, D_MODEL)), _const_spec((1, D_MODEL)), _const_spec((1, KV_WIDTH)), _const_spec((1, D_MODEL)),
        _const_spec((D_MODEL, C_END)), _const_spec((256, 256)), _const_spec((4, POOL_GROUP, POOL_GROUP)),
        _const_spec((D_MODEL, D_MODEL)), _const_spec((D_MODEL, D_MODEL)), _const_spec((D_MODEL, D_MODEL)),
        _const_spec((D_MODEL, D_MODEL)), _const_spec((PLE_DIM, D_MODEL)),
    ]
    out_specs = [
        row(D_MODEL),
        pl.BlockSpec((1, WINDOW, KV_WIDTH), lambda b, t: (b, 0, 0)),
        pl.BlockSpec((1, WINDOW, KV_WIDTH), lambda b, t: (b, 0, 0)),
        pl.BlockSpec((1, POOL_STATE, D_MODEL), lambda b, t: (b, 0, 0)),
    ]
    out_shape = [
        jax.ShapeDtypeStruct((B, T, D_MODEL), F32),
        jax.ShapeDtypeStruct((B, WINDOW, KV_WIDTH), F32),
        jax.ShapeDtypeStruct((B, WINDOW, KV_WIDTH), F32),
        jax.ShapeDtypeStruct((B, POOL_STATE, D_MODEL), F32),
    ]
    scratch = [
        pltpu.VMEM((N_KV_HEADS, (nqb + 1) * 256, LANES), BF16),
        pltpu.VMEM((N_KV_HEADS, (nqb + 1) * 256, 2 * LANES), BF16),
        pltpu.VMEM((TQ + 16, D_MODEL), F32),
        pltpu.VMEM((TQ + 16, D_MODEL), F32),
        pltpu.VMEM((TQ + 16, 768), F32),
        pltpu.VMEM((TQ + 16, 512), F32),
        pltpu.VMEM((TQ, D_MODEL), F32),
    ]
    return pl.pallas_call(
        _prompt_kernel,
        grid=(B, nt),
        in_specs=in_specs,
        out_specs=out_specs,
        out_shape=out_shape,
        scratch_shapes=scratch,
        compiler_params=pltpu.CompilerParams(dimension_semantics=("arbitrary", "arbitrary"),
                                             vmem_limit_bytes=VMEM_LIMIT),
        name="prompt_layer",
    )(w["sinks2"], h, p, tabs[0], tabs[1], tabs[2], bias,
      w["norm_g"], w["gq"], w["gk"], w["pool_scale"], w["w_in"], w["hn"], w["pool_map"],
      w["w_pp"], w["w_pa"], w["w_out"], w["w_g"], w["w_ple"])


def _sample_kernel(sinks_ref, h_ref, p_ref, st_ref, ck_ref, cv_ref, cos_ref, s1_ref, s2_ref, bmain_ref, bnew_ref,
                   ng_ref, gq_ref, gk_ref, ps_ref, win_ref, hn_ref, pm_ref,
                   wpp_ref, wpa_ref, wout_ref, wg_ref, wple_ref,
                   y_ref, nk_ref, nv_ref, nu_ref,
                   q_s, o_s, ya_s, *, nb):
    i = pl.program_id(0)
    nsteps = pl.num_programs(0)
    dec = h_ref.shape[0] // nb
    lane2 = lax.broadcasted_iota(jnp.int32, (1, 2 * LANES), 1) // HEAD_DIM

    @pl.when(i == 0)
    def _():
        h = h_ref[...]
        xn = _rms_in(h, ng_ref[...])

        def proj(c0, c1):
            return _dot(xn, win_ref[:, c0:c1])

        u = proj(C_U, C_ZP)
        nu_ref[...] = u
        seq = [st_ref[j] for j in range(POOL_STATE)] + [u[tt * nb:(tt + 1) * nb] for tt in range(dec)]
        r_rows = []
        for tt in range(dec):
            e = POOL_STATE + tt
            cols = []
            for g, w in enumerate(POOL_WINDOWS):
                cs = slice(g * POOL_GROUP, (g + 1) * POOL_GROUP)
                acc = seq[e][:, cs]
                for d in range(1, w):
                    acc = acc + seq[e - d][:, cs]
                cols.append(acc * (1.0 / w))
            r_rows.append(jnp.concatenate(cols, axis=1) - seq[e])
        r = jnp.concatenate(r_rows, axis=0)
        zp = proj(C_ZP, C_Q)
        ya_s[...] = _pool_map(r, pm_ref, ps_ref[...]) * (zp * _sigmoid(zp))

        cos, s1, s2t = cos_ref[...], s1_ref[...], s2_ref[...]
        hn = hn_ref[...]
        q_s[...] = _rope(_head_norm(proj(C_Q, C_K), hn, gq_ref[...]), cos, s1, s2t)
        nk_ref[...] = _rope(_head_norm(proj(C_K, C_V), hn, gk_ref[...]), cos, s1, s2t)
        nv_ref[...] = proj(C_V, C_ZA)

    b0 = pl.multiple_of(i * SB, SB)
    pieces = []
    for kh in range(N_KV_HEADS):
        for g in range(GQA_GROUP):
            qg = jnp.concatenate([q_s[pl.ds(pl.multiple_of(tt * nb + b0, SB), SB), g * 256:(g + 1) * 256]
                                  for tt in range(dec)], axis=0)
            pieces.append(jnp.where(lane2 == kh, qg, 0.0))
    lhs = jnp.concatenate(pieces, axis=0).astype(BF16)
    rpp = dec * SB
    nrow = N_HEADS * rpp
    kmain = ck_ref[...].reshape(SB * WINDOW, KV_WIDTH).astype(BF16)
    vmain = cv_ref[...].reshape(SB * WINDOW, KV_WIDTH).astype(BF16)
    knew = jnp.concatenate([nk_ref[pl.ds(pl.multiple_of(tt * nb + b0, SB), SB), :] for tt in range(dec)],
                           axis=0).astype(BF16)
    vnew = jnp.concatenate([nv_ref[pl.ds(pl.multiple_of(tt * nb + b0, SB), SB), :] for tt in range(dec)],
                           axis=0).astype(BF16)
    s_main = (_dot_t(lhs, kmain).reshape(N_HEADS, rpp, SB * WINDOW) + bmain_ref[...][None]).reshape(nrow, SB * WINDOW)
    s_new = (_dot_t(lhs, knew).reshape(N_HEADS, rpp, rpp) + bnew_ref[...][None]).reshape(nrow, rpp)
    sink = jnp.concatenate([jnp.full((rpp, 1), sinks_ref[kh * GQA_GROUP + g], F32)
                            for kh in range(N_KV_HEADS) for g in range(GQA_GROUP)], axis=0)
    mx = jnp.maximum(jnp.maximum(jnp.max(s_main, axis=-1, keepdims=True), jnp.max(s_new, axis=-1, keepdims=True)), sink)
    p_main = jnp.exp2(s_main - mx).astype(BF16)
    p_new = jnp.exp2(s_new - mx).astype(BF16)
    den = (jnp.sum(p_main.astype(F32), axis=-1, keepdims=True) + jnp.sum(p_new.astype(F32), axis=-1, keepdims=True)
           + jnp.exp2(sink - mx))
    o = (_dot(p_main, vmain) + _dot(p_new, vnew)) / den
    for g in range(GQA_GROUP):
        sel = None
        for kh in range(N_KV_HEADS):
            piece = o[(kh * GQA_GROUP + g) * rpp:(kh * GQA_GROUP + g + 1) * rpp]
            piece = jnp.where(lane2 == kh, piece, 0.0)
            sel = piece if sel is None else sel + piece
        for tt in range(dec):
            o_s[pl.ds(pl.multiple_of(tt * nb + b0, SB), SB), g * 256:(g + 1) * 256] = sel[tt * SB:(tt + 1) * SB]

    @pl.when(i == nsteps - 1)
    def _():
        h = h_ref[...]
        xn = _rms_in(h, ng_ref[...])
        za = _dot(xn, win_ref[:, C_ZA:C_GA])
        yb = o_s[...] * (za * _sigmoid(za))
        y_ref[...] = _tail(h, p_ref[...], ya_s[...], yb, _dot(xn, win_ref[:, C_GA:C_GB]), _dot(xn, win_ref[:, C_GB:C_END]),
                           wpp_ref[...], wpa_ref[...], wout_ref[...], wg_ref[...], wple_ref[...])


def _sample_layer(h, p, st, ck, cv, tabs, bmain, bnew, w):
    R = h.shape[0]
    nb = ck.shape[0]
    full = lambda shape: pl.BlockSpec(shape, lambda i: (0,) * len(shape), pipeline_mode=pl.Buffered(1))
    cache = pl.BlockSpec((SB, WINDOW, KV_WIDTH), lambda i: (i, 0, 0))
    in_specs = [
        pl.BlockSpec(memory_space=pltpu.SMEM),
        full((R, D_MODEL)), full((R, PLE_DIM)), full((POOL_STATE, nb, D_MODEL)), cache, cache,
        full((R, LANES)), full((R, LANES)), full((R, LANES)), full(bmain.shape), full(bnew.shape),
        full((1, D_MODEL)), full((1, D_MODEL)), full((1, KV_WIDTH)), full((1, D_MODEL)),
        full((D_MODEL, C_END)), full((256, 256)), full((4, POOL_GROUP, POOL_GROUP)),
        full((D_MODEL, D_MODEL)), full((D_MODEL, D_MODEL)), full((D_MODEL, D_MODEL)),
        full((D_MODEL, D_MODEL)), full((PLE_DIM, D_MODEL)),
    ]
    outfull = lambda shape: pl.BlockSpec(shape, lambda i: (0,) * len(shape))
    out_specs = [outfull((R, D_MODEL)), outfull((R, KV_WIDTH)), outfull((R, KV_WIDTH)), outfull((R, D_MODEL))]
    out_shape = [jax.ShapeDtypeStruct((R, D_MODEL), F32), jax.ShapeDtypeStruct((R, KV_WIDTH), F32),
                 jax.ShapeDtypeStruct((R, KV_WIDTH), F32), jax.ShapeDtypeStruct((R, D_MODEL), F32)]
    scratch = [pltpu.VMEM((R, D_MODEL), F32) for _ in range(3)]
    return pl.pallas_call(
        functools.partial(_sample_kernel, nb=nb),
        grid=(nb // SB,),
        in_specs=in_specs,
        out_specs=out_specs,
        out_shape=out_shape,
        scratch_shapes=scratch,
        compiler_params=pltpu.CompilerParams(dimension_semantics=("arbitrary",), vmem_limit_bytes=VMEM_LIMIT),
        name="sample_layer",
    )(w["sinks2"], h, p, st, ck, cv, tabs[0], tabs[1], tabs[2], bmain, bnew,
      w["norm_g"], w["gq"], w["gk"], w["pool_scale"], w["w_in_s"], w["hn"], w["pool_map"],
      w["w_pp"], w["w_pa_s"], w["w_out"], w["w_g"], w["w_ple"])


def _rope_tables(pos):
    inv_freq = ROPE_THETA ** (-jnp.arange(0, ROPE_DIM, 2, dtype=F32) / ROPE_DIM)
    ang = pos.astype(F32)[:, None] * inv_freq[None, :]
    cos, sin = jnp.cos(ang), jnp.sin(ang)
    n = pos.shape[0]
    half = ROPE_DIM // 2
    c64 = jnp.concatenate([cos, cos, jnp.ones((n, HEAD_DIM - ROPE_DIM), F32)], axis=1)
    a64 = jnp.concatenate([-sin, jnp.zeros((n, HEAD_DIM - half), F32)], axis=1)
    b64 = jnp.concatenate([jnp.zeros((n, half), F32), sin, jnp.zeros((n, HEAD_DIM - ROPE_DIM), F32)], axis=1)
    rep = LANES // HEAD_DIM
    return tuple(jnp.tile(x, (1, rep)) for x in (c64, a64, b64))


def _prompt_bias():
    q = np.arange(WINDOW)[:, None]
    j = np.arange(WINDOW)[None, :]
    prev = np.where(j >= q, 0.0, NEG)
    cur = np.where(j <= q, 0.0, NEG)
    normal = np.concatenate([prev, cur], axis=1)
    first = np.concatenate([np.full_like(prev, NEG), cur], axis=1)
    return jnp.asarray(np.stack([normal, first]), F32)


def _sample_bias(dec):
    rq = np.arange(dec * SB)
    tq, bq = rq // SB, rq % SB
    cm = np.arange(SB * WINDOW)
    bm, jm = cm // WINDOW, cm % WINDOW
    main = np.where((bq[:, None] == bm[None, :]) & (jm[None, :] >= tq[:, None]), 0.0, NEG)
    new = np.where((bq[:, None] == bq[None, :]) & (tq[None, :] <= tq[:, None]), 0.0, NEG)
    return jnp.asarray(main, F32), jnp.asarray(new, F32)


def _layer_weights(i, norm_g, w_in, q_norm_g, k_norm_g, sinks, pool_map, pool_scale,
                   w_proj_pool, w_proj_attn, w_out, w_ple, w_ple_gate):
    qscale = HEAD_DIM ** -0.5 * LOG2E
    win = w_in[i]
    perm = lambda m: m.reshape(D_MODEL, N_KV_HEADS, GQA_GROUP, HEAD_DIM).transpose(0, 2, 1, 3).reshape(D_MODEL, D_MODEL)
    win_s = jnp.concatenate([win[:, :C_Q], perm(win[:, C_Q:C_K]), win[:, C_K:C_ZA],
                             perm(win[:, C_ZA:C_GA]), win[:, C_GA:]], axis=1)
    wpa = w_proj_attn[i]
    wpa_s = wpa.reshape(N_KV_HEADS, GQA_GROUP, HEAD_DIM, D_MODEL).transpose(1, 0, 2, 3).reshape(D_MODEL, D_MODEL)
    blk = np.kron(np.eye(256 // HEAD_DIM), np.full((HEAD_DIM, HEAD_DIM), 1.0 / HEAD_DIM))
    return dict(
        sinks2=sinks[i] * LOG2E,
        norm_g=norm_g[i][None, :],
        gq=jnp.tile(q_norm_g[i] * qscale, N_HEADS)[None, :],
        gk=jnp.tile(k_norm_g[i], N_KV_HEADS)[None, :],
        pool_scale=pool_scale[i][None, :],
        w_in=win.astype(BF16), w_in_s=win_s.astype(BF16),
        hn=jnp.asarray(blk, BF16),
        pool_map=pool_map[i].astype(BF16),
        w_pp=w_proj_pool[i].astype(BF16), w_pa=wpa.astype(BF16), w_pa_s=wpa_s.astype(BF16),
        w_out=w_out[i].astype(BF16), w_g=w_ple_gate[i].astype(BF16), w_ple=w_ple[i].astype(BF16),
    )


def kernel(x_prompt, x_sample, cache_k, cache_v, state_pool, p_prompt, p_sample, norm_g, w_in, q_norm_g, k_norm_g, sinks, pool_map, pool_scale, w_proj_pool, w_proj_attn, w_out, w_ple, w_ple_gate):
    depth = w_in.shape[0]
    B, T, _ = x_prompt.shape
    nb, dec, _ = x_sample.shape
    assert T % TQ == 0 and TQ % WINDOW == 0 and nb % SB == 0
    ptabs = _rope_tables(jnp.arange(T, dtype=jnp.int32))
    stabs = _rope_tables(jnp.repeat(PAST_LEN + jnp.arange(dec, dtype=jnp.int32), nb))
    pbias = _prompt_bias()
    bmain, bnew = _sample_bias(dec)

    hp = x_prompt
    hs = x_sample.transpose(1, 0, 2).reshape(dec * nb, D_MODEL)
    kp_l, vp_l, pp_l, ks_l, vs_l, ps_l = [], [], [], [], [], []
    for i in range(depth):
        w = _layer_weights(i, norm_g, w_in, q_norm_g, k_norm_g, sinks, pool_map, pool_scale,
                           w_proj_pool, w_proj_attn, w_out, w_ple, w_ple_gate)
        hp, nk, nv, npool = _prompt_layer(hp, p_prompt[i], ptabs, pbias, w)
        kp_l.append(nk.reshape(B, WINDOW, N_KV_HEADS, HEAD_DIM))
        vp_l.append(nv.reshape(B, WINDOW, N_KV_HEADS, HEAD_DIM))
        pp_l.append(npool)

        ck = cache_k[i].reshape(nb, WINDOW, KV_WIDTH)
        cv = cache_v[i].reshape(nb, WINDOW, KV_WIDTH)
        ps_t = p_sample[i].transpose(1, 0, 2).reshape(dec * nb, PLE_DIM)
        st_t = state_pool[i].transpose(1, 0, 2)
        hs, knew, vnew, unew = _sample_layer(hs, ps_t, st_t, ck, cv, stabs, bmain, bnew, w)
        unt = lambda a: a.reshape(dec, nb, N_KV_HEADS, HEAD_DIM).transpose(1, 0, 2, 3)
        ks_l.append(jnp.concatenate([cache_k[i][:, dec:], unt(knew)], axis=1))
        vs_l.append(jnp.concatenate([cache_v[i][:, dec:], unt(vnew)], axis=1))
        ps_l.append(jnp.concatenate([state_pool[i][:, dec:], unew.reshape(dec, nb, D_MODEL).transpose(1, 0, 2)], axis=1))
    y_sample = hs.reshape(dec, nb, D_MODEL).transpose(1, 0, 2)
    return (hp, y_sample, jnp.stack(kp_l), jnp.stack(vp_l), jnp.stack(pp_l),
            jnp.stack(ks_l), jnp.stack(vs_l), jnp.stack(ps_l))
```

```python
import functools
import math

import numpy as np
import jax
import jax.numpy as jnp
from jax import lax
from jax.experimental import pallas as pl
from jax.experimental.pallas import tpu as pltpu

D_MODEL = 1024
PLE_DIM = 256
POOL_WINDOWS = (2, 4, 8, 16)
POOL_GROUP = D_MODEL // len(POOL_WINDOWS)
POOL_STATE = max(POOL_WINDOWS) - 1
HEAD_DIM = 64
N_HEADS = D_MODEL // HEAD_DIM
N_KV_HEADS = 4
GQA_GROUP = N_HEADS // N_KV_HEADS
KV_WIDTH = N_KV_HEADS * HEAD_DIM
WINDOW = 128
ROPE_THETA = 500000.0
ROPE_DIM = HEAD_DIM // 4
EPS = 1e-6
PAST_LEN = 16384

C_U, C_ZP, C_Q, C_K, C_V, C_ZA, C_GA, C_GB, C_END = 0, 1024, 2048, 3072, 3328, 3584, 4608, 5632, 6656

LANES = 128
SUBLANES = 8
NEG = -1e30
LOG2E = math.log2(math.e)
VMEM_LIMIT = 60 * 1024 * 1024

TQ = 256
SB = 8

F32 = jnp.float32
BF16 = jnp.bfloat16


def _sigmoid(x):
    return 1.0 / (1.0 + jnp.exp2(x * (-LOG2E)))


def _dot(a, b):
    return jnp.dot(a, b, preferred_element_type=F32)


def _dot_t(a, b):
    return lax.dot_general(a, b, (((1,), (1,)), ((), ())), preferred_element_type=F32)


def _rope(x, cos, s1, s2):
    outs = []
    for c in range(x.shape[1] // LANES):
        xc = x[:, c * LANES:(c + 1) * LANES]
        outs.append(xc * cos + pltpu.roll(xc, LANES - ROPE_DIM // 2, 1) * s1 + pltpu.roll(xc, ROPE_DIM // 2, 1) * s2)
    return outs[0] if len(outs) == 1 else jnp.concatenate(outs, axis=1)


def _head_norm(x, hn, gain):
    sq = (x * x).astype(BF16)
    ms = jnp.concatenate([_dot(sq[:, c * 256:(c + 1) * 256], hn) for c in range(x.shape[1] // 256)], axis=1)
    return x * lax.rsqrt(ms + EPS) * gain


def _rms_in(h, g):
    ms = jnp.mean(h * h, axis=-1, keepdims=True)
    return (h * lax.rsqrt(ms + EPS) * g).astype(BF16)


def _tail(h, p, ya, yb, ga, gb, wpp, wpa, wout, wg, wple):
    m = _sigmoid(ga) * _dot(ya.astype(BF16), wpp) + _sigmoid(gb) * _dot(yb.astype(BF16), wpa)
    h1 = h + _dot(m.astype(BF16), wout)
    gate = _sigmoid(_dot(h1.astype(BF16), wg))
    return h1 + gate * _dot(p.astype(BF16), wple)


def _pool_map(r, pm_ref, ps):
    rb = r.astype(BF16)
    mapped = jnp.concatenate([_dot(rb[:, g * POOL_GROUP:(g + 1) * POOL_GROUP], pm_ref[g])
                              for g in range(len(POOL_WINDOWS))], axis=1)
    return mapped * ps


def _prompt_kernel(sinks_ref, h_ref, p_ref, cos_ref, s1_ref, s2_ref, bias_ref,
                   ng_ref, gq_ref, gk_ref, ps_ref, win_ref, hn_ref, pm_ref,
                   wpp_ref, wpa_ref, wout_ref, wg_ref, wple_ref,
                   y_ref, nk_ref, nv_ref, npool_ref,
                   kall, vall, u_s, s2_s, s4_s, s8_s, o_s):
    t = pl.program_id(1)
    nqb = TQ // WINDOW
    lane = lax.broadcasted_iota(jnp.int32, (WINDOW, LANES), 1)
    lo = lane < HEAD_DIM

    @pl.when(t == 0)
    def _():
        kall[...] = jnp.zeros_like(kall)
        vall[:, :, 0:LANES] = jnp.zeros((N_KV_HEADS, (nqb + 1) * 256, LANES), BF16)
        ones_pat = jnp.concatenate([jnp.where(lo, 1.0, 0.0), jnp.where(lo, 0.0, 1.0)], axis=0).astype(BF16)
        for kh in range(N_KV_HEADS):
            for x in range(nqb + 1):
                vall[kh, x * 256:(x + 1) * 256, LANES:2 * LANES] = ones_pat
        u_s[0:16, :] = jnp.zeros((16, D_MODEL), F32)
        s2_s[0:16, :] = jnp.zeros((16, D_MODEL), F32)
        s4_s[0:16, :] = jnp.zeros((16, 768), F32)
        s8_s[0:16, :] = jnp.zeros((16, 512), F32)

    @pl.when(t > 0)
    def _():
        for kh in range(N_KV_HEADS):
            kall[kh, 0:256, :] = kall[kh, nqb * 256:(nqb + 1) * 256, :]
            vall[kh, 0:256, 0:LANES] = vall[kh, nqb * 256:(nqb + 1) * 256, 0:LANES]

    h = h_ref[0]
    xn = _rms_in(h, ng_ref[...])

    def proj(c0, c1):
        return _dot(xn, win_ref[:, c0:c1])

    u = proj(C_U, C_ZP)
    u_s[16:16 + TQ, :] = u
    s2 = u + u_s[15:15 + TQ, :]
    s2_s[16:16 + TQ, :] = s2
    s4 = s2[:, 256:] + s2_s[14:14 + TQ, 256:]
    s4_s[16:16 + TQ, :] = s4
    s8 = s4[:, 256:] + s4_s[12:12 + TQ, 256:]
    s8_s[16:16 + TQ, :] = s8
    s16 = s8[:, 256:] + s8_s[8:8 + TQ, 256:]
    pos1 = (t * TQ + 1 + lax.broadcasted_iota(jnp.int32, (TQ, 1), 0)).astype(F32)
    wins = (s2[:, :256], s4[:, :256], s8[:, :256], s16)
    r = jnp.concatenate([wins[g] * (1.0 / jnp.minimum(pos1, float(w))) for g, w in enumerate(POOL_WINDOWS)],
                        axis=1) - u
    npool_ref[0] = u_s[TQ + 1:TQ + 16, :]
    u_s[0:16, :] = u_s[TQ:TQ + 16, :]
    s2_s[0:16, :] = s2_s[TQ:TQ + 16, :]
    s4_s[0:16, :] = s4_s[TQ:TQ + 16, :]
    s8_s[0:16, :] = s8_s[TQ:TQ + 16, :]
    zp = proj(C_ZP, C_Q)
    ya = _pool_map(r, pm_ref, ps_ref[...]) * (zp * _sigmoid(zp))

    cos, s1, s2t = cos_ref[...], s1_ref[...], s2_ref[...]
    hn = hn_ref[...]
    qb = _rope(_head_norm(proj(C_Q, C_K), hn, gq_ref[...]), cos, s1, s2t).astype(BF16)
    kr = _rope(_head_norm(proj(C_K, C_V), hn, gk_ref[...]), cos, s1, s2t)
    v = proj(C_V, C_ZA)
    nk_ref[0] = kr[TQ - WINDOW:, :]
    nv_ref[0] = v[TQ - WINDOW:, :]

    for x in range(nqb):
        rows = slice(x * WINDOW, (x + 1) * WINDOW)
        base = (x + 1) * 256
        for pr in range(N_KV_HEADS // 2):
            for src, dst, col in ((kr, kall, None), (v, vall, slice(0, LANES))):
                a = src[rows, pr * LANES:(pr + 1) * LANES]
                ra = pltpu.roll(a, HEAD_DIM, 1)
                parts = ((jnp.where(lo, a, 0.0), jnp.where(lo, 0.0, ra)),
                         (jnp.where(lo, ra, 0.0), jnp.where(lo, 0.0, a)))
                for e in range(2):
                    kh = 2 * pr + e
                    lh = jnp.concatenate(parts[e], axis=0).astype(BF16)
                    if col is None:
                        dst[kh, base:base + 256, :] = lh
                    else:
                        dst[kh, base:base + 256, col] = lh

    def tiled_bias(b):
        b4 = jnp.concatenate([b[:, :LANES], b[:, :LANES], b[:, LANES:], b[:, LANES:]], axis=1)
        return jnp.concatenate([b4, b4], axis=0)

    biases = [tiled_bias(jnp.where(t == 0, bias_ref[1], bias_ref[0]))] + [tiled_bias(bias_ref[0])] * (nqb - 1)
    for kh in range(N_KV_HEADS):
        for n in range(nqb):
            rows = slice(n * WINDOW, (n + 1) * WINDOW)
            qs = jnp.concatenate([qb[rows, kh * 256:kh * 256 + LANES],
                                  qb[rows, kh * 256 + LANES:(kh + 1) * 256]], axis=0)
            s = _dot_t(qs, kall[kh, n * 256:n * 256 + 512, :]) + biases[n]
            p_rows, sink_rows = [], []
            for rr in range(2):
                rs = slice(rr * WINDOW, (rr + 1) * WINDOW)
                pcols = [None] * 4
                sterm = []
                for e in range(2):
                    sink = sinks_ref[kh * GQA_GROUP + 2 * rr + e]
                    sp = s[rs, e * LANES:(e + 1) * LANES]
                    sc = s[rs, 256 + e * LANES:256 + (e + 1) * LANES]
                    mx = jnp.maximum(jnp.max(jnp.maximum(sp, sc), axis=-1, keepdims=True), sink)
                    pcols[e] = jnp.exp2(sp - mx)
                    pcols[2 + e] = jnp.exp2(sc - mx)
                    sterm.append(jnp.exp2(sink - mx))
                p_rows.append(jnp.concatenate(pcols, axis=1))
                sink_rows.append(jnp.where(lo, sterm[0], sterm[1]))
            pmat = jnp.concatenate(p_rows, axis=0).astype(BF16)
            o2 = _dot(pmat, vall[kh, n * 256:n * 256 + 512, :])
            o = o2[:, :LANES] / (o2[:, LANES:] + jnp.concatenate(sink_rows, axis=0))
            o_s[rows, kh * 256:kh * 256 + LANES] = o[:WINDOW]
            o_s[rows, kh * 256 + LANES:(kh + 1) * 256] = o[WINDOW:]

    za = proj(C_ZA, C_GA)
    yb = o_s[...] * (za * _sigmoid(za))
    y_ref[0] = _tail(h, p_ref[0], ya, yb, proj(C_GA, C_GB), proj(C_GB, C_END),
                     wpp_ref[...], wpa_ref[...], wout_ref[...], wg_ref[...], wple_ref[...])


def _const_spec(shape):
    nd = len(shape)
    return pl.BlockSpec(shape, lambda *_: (0,) * nd, pipeline_mode=pl.Buffered(1))


def _prompt_layer(h, p, tabs, bias, w):
    B, T, _ = h.shape
    nt = T // TQ
    nqb = TQ // WINDOW
    row = lambda width: pl.BlockSpec((1, TQ, width), lambda b, t: (b, t, 0))
    tab = pl.BlockSpec((TQ, LANES), lambda b, t: (t, 0))
    in_specs = [
        pl.BlockSpec(memory_space=pltpu.SMEM),
        row(D_MODEL), row(PLE_DIM), tab, tab, tab,
        _const_spec(bias.shape),
        _const_spec((1, D_MODEL)), _const_spec((1, D_MODEL)), _const_spec((1, KV_WIDTH)), _const_spec((1, D_MODEL)),
        _const_spec((D_MODEL, C_END)), _const_spec((256, 256)), _const_spec((4, POOL_GROUP, POOL_GROUP)),
        _const_spec((D_MODEL, D_MODEL)), _const_spec((D_MODEL, D_MODEL)), _const_spec((D_MODEL, D_MODEL)),
        _const_spec((D_MODEL, D_MODEL)), _const_spec((PLE_DIM, D_MODEL)),
    ]
    out_specs = [
        row(D_MODEL),
        pl.BlockSpec((1, WINDOW, KV_WIDTH), lambda b, t: (b, 0, 0)),
        pl.BlockSpec((1, WINDOW, KV_WIDTH), lambda b, t: (b, 0, 0)),
        pl.BlockSpec((1, POOL_STATE, D_MODEL), lambda b, t: (b, 0, 0)),
    ]
    out_shape = [
        jax.ShapeDtypeStruct((B, T, D_MODEL), F32),
        jax.ShapeDtypeStruct((B, WINDOW, KV_WIDTH), F32),
        jax.ShapeDtypeStruct((B, WINDOW, KV_WIDTH), F32),
        jax.ShapeDtypeStruct((B, POOL_STATE, D_MODEL), F32),
    ]
    scratch = [
        pltpu.VMEM((N_KV_HEADS, (nqb + 1) * 256, LANES), BF16),
        pltpu.VMEM((N_KV_HEADS, (nqb + 1) * 256, 2 * LANES), BF16),
        pltpu.VMEM((TQ + 16, D_MODEL), F32),
        pltpu.VMEM((TQ + 16, D_MODEL), F32),
        pltpu.VMEM((TQ + 16, 768), F32),
        pltpu.VMEM((TQ + 16, 512), F32),
        pltpu.VMEM((TQ, D_MODEL), F32),
    ]
    return pl.pallas_call(
        _prompt_kernel,
        grid=(B, nt),
        in_specs=in_specs,
        out_specs=out_specs,
        out_shape=out_shape,
        scratch_shapes=scratch,
        compiler_params=pltpu.CompilerParams(dimension_semantics=("arbitrary", "arbitrary"),
                                             vmem_limit_bytes=VMEM_LIMIT),
        name="prompt_layer",
    )(w["sinks2"], h, p, tabs[0], tabs[1], tabs[2], bias,
      w["norm_g"], w["gq"], w["gk"], w["pool_scale"], w["w_in"], w["hn"], w["pool_map"],
      w["w_pp"], w["w_pa"], w["w_out"], w["w_g"], w["w_ple"])


def _sample_kernel(sinks_ref, h_ref, p_ref, st_ref, ck_ref, cv_ref, cos_ref, s1_ref, s2_ref, bmain_ref, bnew_ref,
                   ng_ref, gq_ref, gk_ref, ps_ref, win_ref, hn_ref, pm_ref,
                   wpp_ref, wpa_ref, wout_ref, wg_ref, wple_ref, *rest, nb, n_alias):
    y_ref, nk_ref, nv_ref, nu_ref, nkc_ref, nvc_ref, q_s, o_s, ya_s = rest[n_alias:]
    i = pl.program_id(0)
    nsteps = pl.num_programs(0)
    dec = h_ref.shape[0] // nb

    @pl.when(i == 0)
    def _():
        h = h_ref[...]
        xn = _rms_in(h, ng_ref[...])

        def proj(c0, c1):
            return _dot(xn, win_ref[:, c0:c1])

        nu_ref[...] = proj(C_U, C_ZP)
        zp = proj(C_ZP, C_Q)
        ya_s[...] = zp * _sigmoid(zp)

        cos, s1, s2t = cos_ref[...], s1_ref[...], s2_ref[...]
        hn = hn_ref[...]
        q_s[...] = _rope(_head_norm(proj(C_Q, C_K), hn, gq_ref[...]), cos, s1, s2t)
        nk_ref[...] = _rope(_head_norm(proj(C_K, C_V), hn, gk_ref[...]), cos, s1, s2t)
        nv_ref[...] = proj(C_V, C_ZA)

    b0 = pl.multiple_of(i * SB, SB)
    rpp = dec * SB
    nrow = N_HEADS * rpp
    lo = lax.broadcasted_iota(jnp.int32, (rpp, LANES), 1) < HEAD_DIM
    zero = jnp.zeros((rpp, LANES), F32)

    def rows_tb(ref, c0, c1):
        return jnp.concatenate([ref[pl.ds(pl.multiple_of(tt * nb + b0, SB), SB), c0:c1] for tt in range(dec)], axis=0)

    u_tb = rows_tb(nu_ref, 0, D_MODEL)
    seq = [st_ref[j] for j in range(POOL_STATE)] + [u_tb[tt * SB:(tt + 1) * SB] for tt in range(dec)]
    r_rows = []
    for tt in range(dec):
        e = POOL_STATE + tt
        cols = []
        for g, w in enumerate(POOL_WINDOWS):
            cs = slice(g * POOL_GROUP, (g + 1) * POOL_GROUP)
            acc = seq[e][:, cs]
            for d in range(1, w):
                acc = acc + seq[e - d][:, cs]
            cols.append(acc * (1.0 / w))
        r_rows.append(jnp.concatenate(cols, axis=1) - seq[e])
    ya_tb = _pool_map(jnp.concatenate(r_rows, axis=0), pm_ref, ps_ref[...])
    for tt in range(dec):
        rs = pl.ds(pl.multiple_of(tt * nb + b0, SB), SB)
        ya_s[rs, :] = ya_s[rs, :] * ya_tb[tt * SB:(tt + 1) * SB]

    pieces = []
    for kh in range(N_KV_HEADS):
        for gp in range(GQA_GROUP // 2):
            c = kh * 2 + gp
            x = rows_tb(q_s, c * LANES, (c + 1) * LANES)
            rx = pltpu.roll(x, HEAD_DIM, 1)
            for e in range(2):
                src = x if e == kh % 2 else rx
                half = jnp.where(lo, src, 0.0) if kh % 2 == 0 else jnp.where(lo, 0.0, src)
                pieces.append(jnp.concatenate([half, zero] if kh < 2 else [zero, half], axis=1))
    lhs = jnp.concatenate(pieces, axis=0).astype(BF16)
    kmain = ck_ref[...].reshape(SB * WINDOW, KV_WIDTH).astype(BF16)
    vmain = cv_ref[...].reshape(SB * WINDOW, KV_WIDTH).astype(BF16)
    knew = rows_tb(nk_ref, 0, KV_WIDTH).astype(BF16)
    vnew = rows_tb(nv_ref, 0, KV_WIDTH).astype(BF16)
    s_main = (_dot_t(lhs, kmain).reshape(N_HEADS, rpp, SB * WINDOW) + bmain_ref[...][None]).reshape(nrow, SB * WINDOW)
    s_new = (_dot_t(lhs, knew).reshape(N_HEADS, rpp, rpp) + bnew_ref[...][None]).reshape(nrow, rpp)
    sink = jnp.concatenate([jnp.full((rpp, 1), sinks_ref[hh], F32) for hh in range(N_HEADS)], axis=0)
    mx = jnp.maximum(jnp.maximum(jnp.max(s_main, axis=-1, keepdims=True), jnp.max(s_new, axis=-1, keepdims=True)), sink)
    p_main = jnp.exp2(s_main - mx).astype(BF16)
    p_new = jnp.exp2(s_new - mx).astype(BF16)
    den = (jnp.sum(p_main.astype(F32), axis=-1, keepdims=True) + jnp.sum(p_new.astype(F32), axis=-1, keepdims=True)
           + jnp.exp2(sink - mx))
    o = (_dot(p_main, vmain) + _dot(p_new, vnew)) / den
    for kh in range(N_KV_HEADS):
        for gp in range(GQA_GROUP // 2):
            halves = []
            for e in range(2):
                hh = kh * GQA_GROUP + 2 * gp + e
                x = o[hh * rpp:(hh + 1) * rpp, (kh // 2) * LANES:(kh // 2 + 1) * LANES]
                halves.append(x if e == kh % 2 else pltpu.roll(x, HEAD_DIM, 1))
            dest = jnp.where(lo, halves[0], halves[1])
            c = kh * 2 + gp
            for tt in range(dec):
                o_s[pl.ds(pl.multiple_of(tt * nb + b0, SB), SB), c * LANES:(c + 1) * LANES] = dest[tt * SB:(tt + 1) * SB]

    for bl in range(SB):
        nkc_ref[bl, 0:WINDOW - dec, :] = ck_ref[bl, dec:WINDOW, :]
        nkc_ref[bl, WINDOW - dec:WINDOW, :] = ck_ref[bl, 0:dec, :]
        nvc_ref[bl, 0:WINDOW - dec, :] = cv_ref[bl, dec:WINDOW, :]
        nvc_ref[bl, WINDOW - dec:WINDOW, :] = cv_ref[bl, 0:dec, :]

    @pl.when(i == nsteps - 1)
    def _():
        h = h_ref[...]
        xn = _rms_in(h, ng_ref[...])
        za = _dot(xn, win_ref[:, C_ZA:C_GA])
        yb = o_s[...] * (za * _sigmoid(za))
        y_ref[...] = _tail(h, p_ref[...], ya_s[...], yb, _dot(xn, win_ref[:, C_GA:C_GB]), _dot(xn, win_ref[:, C_GB:C_END]),
                           wpp_ref[...], wpa_ref[...], wout_ref[...], wg_ref[...], wple_ref[...])


def _sample_layer(h, p, st, ck, cv, tabs, bmain, bnew, w, layer, stacked):
    R = h.shape[0]
    nb = ck.shape[1]
    full = lambda shape: pl.BlockSpec(shape, lambda i: (0,) * len(shape), pipeline_mode=pl.Buffered(1))
    cache = pl.BlockSpec((None, SB, WINDOW, KV_WIDTH), lambda i: (layer, i, 0, 0))
    in_specs = [
        pl.BlockSpec(memory_space=pltpu.SMEM),
        full((R, D_MODEL)), full((R, PLE_DIM)), pl.BlockSpec((POOL_STATE, SB, D_MODEL), lambda i: (0, i, 0)), cache, cache,
        full((R, LANES)), full((R, LANES)), full((R, LANES)), full(bmain.shape), full(bnew.shape),
        full((1, D_MODEL)), full((1, D_MODEL)), full((1, KV_WIDTH)), full((1, D_MODEL)),
        full((D_MODEL, C_END)), full((256, 256)), full((4, POOL_GROUP, POOL_GROUP)),
        full((D_MODEL, D_MODEL)), full((D_MODEL, D_MODEL)), full((D_MODEL, D_MODEL)),
        full((D_MODEL, D_MODEL)), full((PLE_DIM, D_MODEL)),
    ]
    outfull = full
    out_specs = [outfull((R, D_MODEL)), outfull((R, KV_WIDTH)), outfull((R, KV_WIDTH)), outfull((R, D_MODEL)),
                 cache, cache]
    out_shape = [jax.ShapeDtypeStruct((R, D_MODEL), F32), jax.ShapeDtypeStruct((R, KV_WIDTH), F32),
                 jax.ShapeDtypeStruct((R, KV_WIDTH), F32), jax.ShapeDtypeStruct((R, D_MODEL), F32),
                 jax.ShapeDtypeStruct(ck.shape, F32), jax.ShapeDtypeStruct(cv.shape, F32)]
    scratch = [pltpu.VMEM((R, D_MODEL), F32) for _ in range(3)]
    args = [w["sinks2"], h, p, st, ck, cv, tabs[0], tabs[1], tabs[2], bmain, bnew,
            w["norm_g"], w["gq"], w["gk"], w["pool_scale"], w["w_in"], w["hn"], w["pool_map"],
            w["w_pp"], w["w_pa"], w["w_out"], w["w_g"], w["w_ple"]]
    aliases = {}
    if stacked is not None:
        aliases = {len(args): 4, len(args) + 1: 5}
        args += list(stacked)
        in_specs += [pl.BlockSpec(memory_space=pl.ANY)] * 2
    return pl.pallas_call(
        functools.partial(_sample_kernel, nb=nb, n_alias=len(aliases)),
        grid=(nb // SB,),
        in_specs=in_specs,
        out_specs=out_specs,
        out_shape=out_shape,
        scratch_shapes=scratch,
        input_output_aliases=aliases,
        compiler_params=pltpu.CompilerParams(dimension_semantics=("arbitrary",), vmem_limit_bytes=VMEM_LIMIT),
        name="sample_layer",
    )(*args)


def _rope_tables(pos):
    inv_freq = ROPE_THETA ** (-jnp.arange(0, ROPE_DIM, 2, dtype=F32) / ROPE_DIM)
    ang = pos.astype(F32)[:, None] * inv_freq[None, :]
    cos, sin = jnp.cos(ang), jnp.sin(ang)
    n = pos.shape[0]
    half = ROPE_DIM // 2
    c64 = jnp.concatenate([cos, cos, jnp.ones((n, HEAD_DIM - ROPE_DIM), F32)], axis=1)
    a64 = jnp.concatenate([-sin, jnp.zeros((n, HEAD_DIM - half), F32)], axis=1)
    b64 = jnp.concatenate([jnp.zeros((n, half), F32), sin, jnp.zeros((n, HEAD_DIM - ROPE_DIM), F32)], axis=1)
    rep = LANES // HEAD_DIM
    return tuple(jnp.tile(x, (1, rep)) for x in (c64, a64, b64))


def _prompt_bias():
    q = np.arange(WINDOW)[:, None]
    j = np.arange(WINDOW)[None, :]
    prev = np.where(j >= q, 0.0, NEG)
    cur = np.where(j <= q, 0.0, NEG)
    normal = np.concatenate([prev, cur], axis=1)
    first = np.concatenate([np.full_like(prev, NEG), cur], axis=1)
    return jnp.asarray(np.stack([normal, first]), F32)


def _sample_bias(dec):
    rq = np.arange(dec * SB)
    tq, bq = rq // SB, rq % SB
    cm = np.arange(SB * WINDOW)
    bm, jm = cm // WINDOW, cm % WINDOW
    main = np.where((bq[:, None] == bm[None, :]) & (jm[None, :] >= tq[:, None]), 0.0, NEG)
    new = np.where((bq[:, None] == bq[None, :]) & (tq[None, :] <= tq[:, None]), 0.0, NEG)
    return jnp.asarray(main, F32), jnp.asarray(new, F32)


def _layer_weights(i, norm_g, w_in, q_norm_g, k_norm_g, sinks, pool_map, pool_scale,
                   w_proj_pool, w_proj_attn, w_out, w_ple, w_ple_gate):
    qscale = HEAD_DIM ** -0.5 * LOG2E
    blk = np.kron(np.eye(256 // HEAD_DIM), np.full((HEAD_DIM, HEAD_DIM), 1.0 / HEAD_DIM))
    return dict(
        sinks2=sinks[i] * LOG2E,
        norm_g=norm_g[i][None, :],
        gq=jnp.tile(q_norm_g[i] * qscale, N_HEADS)[None, :],
        gk=jnp.tile(k_norm_g[i], N_KV_HEADS)[None, :],
        pool_scale=pool_scale[i][None, :],
        w_in=w_in[i].astype(BF16),
        hn=jnp.asarray(blk, BF16),
        pool_map=pool_map[i].astype(BF16),
        w_pp=w_proj_pool[i].astype(BF16), w_pa=w_proj_attn[i].astype(BF16),
        w_out=w_out[i].astype(BF16), w_g=w_ple_gate[i].astype(BF16), w_ple=w_ple[i].astype(BF16),
    )


def kernel(x_prompt, x_sample, cache_k, cache_v, state_pool, p_prompt, p_sample, norm_g, w_in, q_norm_g, k_norm_g, sinks, pool_map, pool_scale, w_proj_pool, w_proj_attn, w_out, w_ple, w_ple_gate):
    depth = w_in.shape[0]
    B, T, _ = x_prompt.shape
    nb, dec, _ = x_sample.shape
    assert T % TQ == 0 and TQ % WINDOW == 0 and nb % SB == 0
    ptabs = _rope_tables(jnp.arange(T, dtype=jnp.int32))
    stabs = _rope_tables(jnp.repeat(PAST_LEN + jnp.arange(dec, dtype=jnp.int32), nb))
    pbias = _prompt_bias()
    bmain, bnew = _sample_bias(dec)

    hp = x_prompt
    hs = x_sample.transpose(1, 0, 2).reshape(dec * nb, D_MODEL)
    ck_all = cache_k.reshape(depth, nb, WINDOW, KV_WIDTH)
    cv_all = cache_v.reshape(depth, nb, WINDOW, KV_WIDTH)
    stacked = None
    kp_l, vp_l, pp_l, ks_l, vs_l, ps_l = [], [], [], [], [], []
    for i in range(depth):
        w = _layer_weights(i, norm_g, w_in, q_norm_g, k_norm_g, sinks, pool_map, pool_scale,
                           w_proj_pool, w_proj_attn, w_out, w_ple, w_ple_gate)
        hp, nk, nv, npool = _prompt_layer(hp, p_prompt[i], ptabs, pbias, w)
        kp_l.append(nk.reshape(B, WINDOW, N_KV_HEADS, HEAD_DIM))
        vp_l.append(nv.reshape(B, WINDOW, N_KV_HEADS, HEAD_DIM))
        pp_l.append(npool)

        ps_t = p_sample[i].transpose(1, 0, 2).reshape(dec * nb, PLE_DIM)
        st_t = state_pool[i].transpose(1, 0, 2)
        hs, knew, vnew, unew, kc, vc = _sample_layer(hs, ps_t, st_t, ck_all, cv_all, stabs, bmain, bnew, w, i, stacked)
        stacked = (kc, vc)
        unt = lambda a: a.reshape(dec, nb, -1).transpose(1, 0, 2)
        ks_l.append(unt(knew))
        vs_l.append(unt(vnew))
        ps_l.append(jnp.concatenate([state_pool[i][:, dec:], unt(unew)], axis=1))
    new_k_sample = stacked[0].at[:, :, WINDOW - dec:].set(jnp.stack(ks_l)).reshape(cache_k.shape)
    new_v_sample = stacked[1].at[:, :, WINDOW - dec:].set(jnp.stack(vs_l)).reshape(cache_v.shape)
    y_sample = hs.reshape(dec, nb, D_MODEL).transpose(1, 0, 2)
    return (hp, y_sample, jnp.stack(kp_l), jnp.stack(vp_l), jnp.stack(pp_l),
            new_k_sample, new_v_sample, jnp.stack(ps_l))
```

```python
import functools
import math

import numpy as np
import jax
import jax.numpy as jnp
from jax import lax
from jax.experimental import pallas as pl
from jax.experimental.pallas import tpu as pltpu

D_MODEL = 1024
PLE_DIM = 256
POOL_WINDOWS = (2, 4, 8, 16)
POOL_GROUP = D_MODEL // len(POOL_WINDOWS)
POOL_STATE = max(POOL_WINDOWS) - 1
HEAD_DIM = 64
N_HEADS = D_MODEL // HEAD_DIM
N_KV_HEADS = 4
GQA_GROUP = N_HEADS // N_KV_HEADS
KV_WIDTH = N_KV_HEADS * HEAD_DIM
WINDOW = 128
ROPE_THETA = 500000.0
ROPE_DIM = HEAD_DIM // 4
EPS = 1e-6
PAST_LEN = 16384

C_U, C_ZP, C_Q, C_K, C_V, C_ZA, C_GA, C_GB, C_END = 0, 1024, 2048, 3072, 3328, 3584, 4608, 5632, 6656

LANES = 128
SUBLANES = 8
NEG = -1e30
LOG2E = math.log2(math.e)
VMEM_LIMIT = 60 * 1024 * 1024

TQ = 256
SB = 8

F32 = jnp.float32
BF16 = jnp.bfloat16


def _sigmoid(x):
    return 1.0 / (1.0 + jnp.exp2(x * (-LOG2E)))


def _dot(a, b):
    return jnp.dot(a, b, preferred_element_type=F32)


def _dot_t(a, b):
    return lax.dot_general(a, b, (((1,), (1,)), ((), ())), preferred_element_type=F32)


def _rope(x, cos, s1, s2):
    outs = []
    for c in range(x.shape[1] // LANES):
        xc = x[:, c * LANES:(c + 1) * LANES]
        outs.append(xc * cos + pltpu.roll(xc, LANES - ROPE_DIM // 2, 1) * s1 + pltpu.roll(xc, ROPE_DIM // 2, 1) * s2)
    return outs[0] if len(outs) == 1 else jnp.concatenate(outs, axis=1)


def _head_norm(x, hn, gain):
    sq = (x * x).astype(BF16)
    ms = jnp.concatenate([_dot(sq[:, c * 256:(c + 1) * 256], hn) for c in range(x.shape[1] // 256)], axis=1)
    return x * lax.rsqrt(ms + EPS) * gain


def _rms_in(h, g):
    ms = jnp.mean(h * h, axis=-1, keepdims=True)
    return (h * lax.rsqrt(ms + EPS) * g).astype(BF16)


def _tail(h, p, ya, yb, ga, gb, wpp, wpa, wout, wg, wple):
    m = _sigmoid(ga) * _dot(ya.astype(BF16), wpp) + _sigmoid(gb) * _dot(yb.astype(BF16), wpa)
    h1 = h + _dot(m.astype(BF16), wout)
    gate = _sigmoid(_dot(h1.astype(BF16), wg))
    return h1 + gate * _dot(p.astype(BF16), wple)


def _pool_map(r, pm_ref, ps):
    rb = r.astype(BF16)
    mapped = jnp.concatenate([_dot(rb[:, g * POOL_GROUP:(g + 1) * POOL_GROUP], pm_ref[g])
                              for g in range(len(POOL_WINDOWS))], axis=1)
    return mapped * ps


def _prompt_kernel(sinks_ref, h_ref, p_ref, cos_ref, s1_ref, s2_ref, bias_ref,
                   ng_ref, gq_ref, gk_ref, ps_ref, win_ref, hn_ref, pm_ref,
                   wpp_ref, wpa_ref, wout_ref, wg_ref, wple_ref,
                   y_ref, nk_ref, nv_ref, npool_ref,
                   kall, vall, u_s, s2_s, s4_s, s8_s, o_s):
    t = pl.program_id(1)
    nqb = TQ // WINDOW
    lane = lax.broadcasted_iota(jnp.int32, (WINDOW, LANES), 1)
    lo = lane < HEAD_DIM

    @pl.when(t == 0)
    def _():
        kall[...] = jnp.zeros_like(kall)
        vall[:, :, 0:LANES] = jnp.zeros((N_KV_HEADS, (nqb + 1) * 256, LANES), BF16)
        ones_pat = jnp.concatenate([jnp.where(lo, 1.0, 0.0), jnp.where(lo, 0.0, 1.0)], axis=0).astype(BF16)
        for kh in range(N_KV_HEADS):
            for x in range(nqb + 1):
                vall[kh, x * 256:(x + 1) * 256, LANES:2 * LANES] = ones_pat
        u_s[0:16, :] = jnp.zeros((16, D_MODEL), F32)
        s2_s[0:16, :] = jnp.zeros((16, D_MODEL), F32)
        s4_s[0:16, :] = jnp.zeros((16, 768), F32)
        s8_s[0:16, :] = jnp.zeros((16, 512), F32)

    @pl.when(t > 0)
    def _():
        for kh in range(N_KV_HEADS):
            kall[kh, 0:256, :] = kall[kh, nqb * 256:(nqb + 1) * 256, :]
            vall[kh, 0:256, 0:LANES] = vall[kh, nqb * 256:(nqb + 1) * 256, 0:LANES]

    h = h_ref[0]
    xn = _rms_in(h, ng_ref[...])

    def proj(c0, c1):
        return _dot(xn, win_ref[:, c0:c1])

    u = proj(C_U, C_ZP)
    u_s[16:16 + TQ, :] = u
    s2 = u + u_s[15:15 + TQ, :]
    s2_s[16:16 + TQ, :] = s2
    s4 = s2[:, 256:] + s2_s[14:14 + TQ, 256:]
    s4_s[16:16 + TQ, :] = s4
    s8 = s4[:, 256:] + s4_s[12:12 + TQ, 256:]
    s8_s[16:16 + TQ, :] = s8
    s16 = s8[:, 256:] + s8_s[8:8 + TQ, 256:]
    pos1 = (t * TQ + 1 + lax.broadcasted_iota(jnp.int32, (TQ, 1), 0)).astype(F32)
    wins = (s2[:, :256], s4[:, :256], s8[:, :256], s16)
    r = jnp.concatenate([wins[g] * (1.0 / jnp.minimum(pos1, float(w))) for g, w in enumerate(POOL_WINDOWS)],
                        axis=1) - u
    npool_ref[0] = u_s[TQ + 1:TQ + 16, :]
    u_s[0:16, :] = u_s[TQ:TQ + 16, :]
    s2_s[0:16, :] = s2_s[TQ:TQ + 16, :]
    s4_s[0:16, :] = s4_s[TQ:TQ + 16, :]
    s8_s[0:16, :] = s8_s[TQ:TQ + 16, :]
    zp = proj(C_ZP, C_Q)
    ya = _pool_map(r, pm_ref, ps_ref[...]) * (zp * _sigmoid(zp))

    cos, s1, s2t = cos_ref[...], s1_ref[...], s2_ref[...]
    hn = hn_ref[...]
    qb = _rope(_head_norm(proj(C_Q, C_K), hn, gq_ref[...]), cos, s1, s2t).astype(BF16)
    kr = _rope(_head_norm(proj(C_K, C_V), hn, gk_ref[...]), cos, s1, s2t)
    v = proj(C_V, C_ZA)
    nk_ref[0] = kr[TQ - WINDOW:, :]
    nv_ref[0] = v[TQ - WINDOW:, :]

    for x in range(nqb):
        rows = slice(x * WINDOW, (x + 1) * WINDOW)
        base = (x + 1) * 256
        for pr in range(N_KV_HEADS // 2):
            for src, dst, col in ((kr, kall, None), (v, vall, slice(0, LANES))):
                a = src[rows, pr * LANES:(pr + 1) * LANES]
                ra = pltpu.roll(a, HEAD_DIM, 1)
                parts = ((jnp.where(lo, a, 0.0), jnp.where(lo, 0.0, ra)),
                         (jnp.where(lo, ra, 0.0), jnp.where(lo, 0.0, a)))
                for e in range(2):
                    kh = 2 * pr + e
                    lh = jnp.concatenate(parts[e], axis=0).astype(BF16)
                    if col is None:
                        dst[kh, base:base + 256, :] = lh
                    else:
                        dst[kh, base:base + 256, col] = lh

    def tiled_bias(b):
        b4 = jnp.concatenate([b[:, :LANES], b[:, :LANES], b[:, LANES:], b[:, LANES:]], axis=1)
        return jnp.concatenate([b4, b4], axis=0)

    biases = [tiled_bias(jnp.where(t == 0, bias_ref[1], bias_ref[0]))] + [tiled_bias(bias_ref[0])] * (nqb - 1)
    for kh in range(N_KV_HEADS):
        for n in range(nqb):
            rows = slice(n * WINDOW, (n + 1) * WINDOW)
            qs = jnp.concatenate([qb[rows, kh * 256:kh * 256 + LANES],
                                  qb[rows, kh * 256 + LANES:(kh + 1) * 256]], axis=0)
            s = _dot_t(qs, kall[kh, n * 256:n * 256 + 512, :]) + biases[n]
            p_rows, sink_rows = [], []
            for rr in range(2):
                rs = slice(rr * WINDOW, (rr + 1) * WINDOW)
                pcols = [None] * 4
                sterm = []
                for e in range(2):
                    sink = sinks_ref[kh * GQA_GROUP + 2 * rr + e]
                    sp = s[rs, e * LANES:(e + 1) * LANES]
                    sc = s[rs, 256 + e * LANES:256 + (e + 1) * LANES]
                    mx = jnp.maximum(jnp.max(jnp.maximum(sp, sc), axis=-1, keepdims=True), sink)
                    pcols[e] = jnp.exp2(sp - mx)
                    pcols[2 + e] = jnp.exp2(sc - mx)
                    sterm.append(jnp.exp2(sink - mx))
                p_rows.append(jnp.concatenate(pcols, axis=1))
                sink_rows.append(jnp.where(lo, sterm[0], sterm[1]))
            pmat = jnp.concatenate(p_rows, axis=0).astype(BF16)
            o2 = _dot(pmat, vall[kh, n * 256:n * 256 + 512, :])
            o = o2[:, :LANES] / (o2[:, LANES:] + jnp.concatenate(sink_rows, axis=0))
            o_s[rows, kh * 256:kh * 256 + LANES] = o[:WINDOW]
            o_s[rows, kh * 256 + LANES:(kh + 1) * 256] = o[WINDOW:]

    za = proj(C_ZA, C_GA)
    yb = o_s[...] * (za * _sigmoid(za))
    y_ref[0] = _tail(h, p_ref[0], ya, yb, proj(C_GA, C_GB), proj(C_GB, C_END),
                     wpp_ref[...], wpa_ref[...], wout_ref[...], wg_ref[...], wple_ref[...])


def _const_spec(shape):
    nd = len(shape)
    return pl.BlockSpec(shape, lambda *_: (0,) * nd, pipeline_mode=pl.Buffered(1))


def _layer_spec(shape, layer):
    nd = len(shape)
    return pl.BlockSpec((None,) + tuple(shape), lambda *_: (layer,) + (0,) * nd, pipeline_mode=pl.Buffered(1))


def _weight_specs(layer):
    ls = lambda *shape: _layer_spec(shape, layer)
    return [ls(1, D_MODEL), ls(1, D_MODEL), ls(1, KV_WIDTH), ls(1, D_MODEL),
            ls(D_MODEL, C_END), _const_spec((256, 256)), ls(4, POOL_GROUP, POOL_GROUP),
            ls(D_MODEL, D_MODEL), ls(D_MODEL, D_MODEL), ls(D_MODEL, D_MODEL), ls(D_MODEL, D_MODEL), ls(PLE_DIM, D_MODEL)]


def _weight_args(w):
    return [w["norm_g"], w["gq"], w["gk"], w["pool_scale"], w["w_in"], w["hn"], w["pool_map"],
            w["w_pp"], w["w_pa"], w["w_out"], w["w_g"], w["w_ple"]]


def _prompt_layer(h, p, tabs, bias, w, layer):
    B, T, _ = h.shape
    nt = T // TQ
    nqb = TQ // WINDOW
    row = lambda width: pl.BlockSpec((1, TQ, width), lambda b, t: (b, t, 0))
    tab = pl.BlockSpec((TQ, LANES), lambda b, t: (t, 0))
    in_specs = [
        pl.BlockSpec(memory_space=pltpu.SMEM),
        row(D_MODEL), pl.BlockSpec((None, 1, TQ, PLE_DIM), lambda b, t: (layer, b, t, 0)), tab, tab, tab,
        _const_spec(bias.shape),
    ] + _weight_specs(layer)
    out_specs = [
        row(D_MODEL),
        pl.BlockSpec((1, WINDOW, KV_WIDTH), lambda b, t: (b, 0, 0)),
        pl.BlockSpec((1, WINDOW, KV_WIDTH), lambda b, t: (b, 0, 0)),
        pl.BlockSpec((1, POOL_STATE, D_MODEL), lambda b, t: (b, 0, 0)),
    ]
    out_shape = [
        jax.ShapeDtypeStruct((B, T, D_MODEL), F32),
        jax.ShapeDtypeStruct((B, WINDOW, KV_WIDTH), F32),
        jax.ShapeDtypeStruct((B, WINDOW, KV_WIDTH), F32),
        jax.ShapeDtypeStruct((B, POOL_STATE, D_MODEL), F32),
    ]
    scratch = [
        pltpu.VMEM((N_KV_HEADS, (nqb + 1) * 256, LANES), BF16),
        pltpu.VMEM((N_KV_HEADS, (nqb + 1) * 256, 2 * LANES), BF16),
        pltpu.VMEM((TQ + 16, D_MODEL), F32),
        pltpu.VMEM((TQ + 16, D_MODEL), F32),
        pltpu.VMEM((TQ + 16, 768), F32),
        pltpu.VMEM((TQ + 16, 512), F32),
        pltpu.VMEM((TQ, D_MODEL), F32),
    ]
    return pl.pallas_call(
        _prompt_kernel,
        grid=(B, nt),
        in_specs=in_specs,
        out_specs=out_specs,
        out_shape=out_shape,
        scratch_shapes=scratch,
        compiler_params=pltpu.CompilerParams(dimension_semantics=("arbitrary", "arbitrary"),
                                             vmem_limit_bytes=VMEM_LIMIT),
        name="prompt_layer",
    )(w["sinks2"][layer], h, p, tabs[0], tabs[1], tabs[2], bias, *_weight_args(w))


def _split3(x):
    hi = x.astype(BF16)
    r1 = x - hi.astype(F32)
    mid = r1.astype(BF16)
    return hi, mid, (r1 - mid.astype(F32)).astype(BF16)


def _sample_kernel(sinks_ref, h_ref, p_ref, st_ref, ck_ref, cv_ref, cos_ref, s1_ref, s2_ref, bmain_ref, bnew_ref, sel_ref,
                   ng_ref, gq_ref, gk_ref, ps_ref, win_ref, hn_ref, pm_ref,
                   wpp_ref, wpa_ref, wout_ref, wg_ref, wple_ref, *rest, nb, n_alias):
    y_ref, nkc_ref, nvc_ref, npool_ref, q_s, o_s, ya_s, nu_ref, nk_ref, nv_ref = rest[n_alias:]
    i = pl.program_id(0)
    nsteps = pl.num_programs(0)
    dec = h_ref.shape[0] // nb

    @pl.when(i == 0)
    def _():
        h = h_ref[...]
        xn = _rms_in(h, ng_ref[...])

        def proj(c0, c1):
            return _dot(xn, win_ref[:, c0:c1])

        nu_ref[...] = proj(C_U, C_ZP)
        zp = proj(C_ZP, C_Q)
        ya_s[...] = zp * _sigmoid(zp)

        cos, s1, s2t = cos_ref[...], s1_ref[...], s2_ref[...]
        hn = hn_ref[...]
        q_s[...] = _rope(_head_norm(proj(C_Q, C_K), hn, gq_ref[...]), cos, s1, s2t)
        nk_ref[...] = _rope(_head_norm(proj(C_K, C_V), hn, gk_ref[...]), cos, s1, s2t)
        nv_ref[...] = proj(C_V, C_ZA)

    b0 = pl.multiple_of(i * SB, SB)
    rpp = dec * SB
    nrow = N_HEADS * rpp
    lo = lax.broadcasted_iota(jnp.int32, (rpp, LANES), 1) < HEAD_DIM
    zero = jnp.zeros((rpp, LANES), F32)

    def rows_tb(ref, c0, c1):
        return jnp.concatenate([ref[pl.ds(pl.multiple_of(tt * nb + b0, SB), SB), c0:c1] for tt in range(dec)], axis=0)

    u_tb = rows_tb(nu_ref, 0, D_MODEL)
    seq = [st_ref[j] for j in range(POOL_STATE)] + [u_tb[tt * SB:(tt + 1) * SB] for tt in range(dec)]
    r_rows = []
    for tt in range(dec):
        e = POOL_STATE + tt
        cols = []
        for g, w in enumerate(POOL_WINDOWS):
            cs = slice(g * POOL_GROUP, (g + 1) * POOL_GROUP)
            acc = seq[e][:, cs]
            for d in range(1, w):
                acc = acc + seq[e - d][:, cs]
            cols.append(acc * (1.0 / w))
        r_rows.append(jnp.concatenate(cols, axis=1) - seq[e])
    ya_tb = _pool_map(jnp.concatenate(r_rows, axis=0), pm_ref, ps_ref[...])
    for tt in range(dec):
        rs = pl.ds(pl.multiple_of(tt * nb + b0, SB), SB)
        ya_s[rs, :] = ya_s[rs, :] * ya_tb[tt * SB:(tt + 1) * SB]

    pieces = []
    for kh in range(N_KV_HEADS):
        for gp in range(GQA_GROUP // 2):
            c = kh * 2 + gp
            x = rows_tb(q_s, c * LANES, (c + 1) * LANES)
            rx = pltpu.roll(x, HEAD_DIM, 1)
            for e in range(2):
                src = x if e == kh % 2 else rx
                half = jnp.where(lo, src, 0.0) if kh % 2 == 0 else jnp.where(lo, 0.0, src)
                pieces.append(jnp.concatenate([half, zero] if kh < 2 else [zero, half], axis=1))
    lhs = jnp.concatenate(pieces, axis=0).astype(BF16)
    kmain = jnp.concatenate([ck_ref[bl].astype(BF16) for bl in range(SB)], axis=1)
    vmain = jnp.concatenate([cv_ref[bl].astype(BF16) for bl in range(SB)], axis=1)
    knew_f = rows_tb(nk_ref, 0, KV_WIDTH)
    vnew_f = rows_tb(nv_ref, 0, KV_WIDTH)
    knew = knew_f.astype(BF16)
    vnew = vnew_f.astype(BF16)
    s_main = (_dot(lhs, kmain).reshape(N_HEADS, rpp, SB * WINDOW) + bmain_ref[...][None]).reshape(nrow, SB * WINDOW)
    s_new = (_dot_t(lhs, knew).reshape(N_HEADS, rpp, rpp) + bnew_ref[...][None]).reshape(nrow, rpp)
    sink = jnp.concatenate([jnp.full((rpp, 1), sinks_ref[hh], F32) for hh in range(N_HEADS)], axis=0)
    mx = jnp.maximum(jnp.maximum(jnp.max(s_main, axis=-1, keepdims=True), jnp.max(s_new, axis=-1, keepdims=True)), sink)
    p_main = jnp.exp2(s_main - mx).astype(BF16)
    p_new = jnp.exp2(s_new - mx).astype(BF16)
    den = (jnp.sum(p_main.astype(F32), axis=-1, keepdims=True) + jnp.sum(p_new.astype(F32), axis=-1, keepdims=True)
           + jnp.exp2(sink - mx))
    o = (_dot_t(p_main, vmain) + _dot(p_new, vnew)) / den
    for kh in range(N_KV_HEADS):
        for gp in range(GQA_GROUP // 2):
            halves = []
            for e in range(2):
                hh = kh * GQA_GROUP + 2 * gp + e
                x = o[hh * rpp:(hh + 1) * rpp, (kh // 2) * LANES:(kh // 2 + 1) * LANES]
                halves.append(x if e == kh % 2 else pltpu.roll(x, HEAD_DIM, 1))
            dest = jnp.where(lo, halves[0], halves[1])
            c = kh * 2 + gp
            for tt in range(dec):
                o_s[pl.ds(pl.multiple_of(tt * nb + b0, SB), SB), c * LANES:(c + 1) * LANES] = dest[tt * SB:(tt + 1) * SB]

    slot = lax.broadcasted_iota(jnp.int32, (KV_WIDTH, WINDOW), 1)
    for src_ref, new_f, dst_ref in ((ck_ref, knew_f, nkc_ref), (cv_ref, vnew_f, nvc_ref)):
        pieces3 = _split3(new_f)
        for bl in range(SB):
            cols = None
            for pc in pieces3:
                part = lax.dot_general(pc, sel_ref[bl], (((0,), (0,)), ((), ())), preferred_element_type=F32)
                cols = part if cols is None else cols + part
            dst_ref[bl] = jnp.where(slot < WINDOW - dec, pltpu.roll(src_ref[bl], WINDOW - dec, 1), cols)

    for j in range(POOL_STATE):
        npool_ref[j] = seq[j + dec]

    @pl.when(i == nsteps - 1)
    def _():
        h = h_ref[...]
        xn = _rms_in(h, ng_ref[...])
        za = _dot(xn, win_ref[:, C_ZA:C_GA])
        yb = o_s[...] * (za * _sigmoid(za))
        y_ref[...] = _tail(h, p_ref[...], ya_s[...], yb, _dot(xn, win_ref[:, C_GA:C_GB]), _dot(xn, win_ref[:, C_GB:C_END]),
                           wpp_ref[...], wpa_ref[...], wout_ref[...], wg_ref[...], wple_ref[...])


def _sample_layer(h, p, st, ck, cv, tabs, bmain, bnew, sel, w, layer, stacked):
    R = h.shape[0]
    nb = ck.shape[1]
    full = lambda shape: pl.BlockSpec(shape, lambda i: (0,) * len(shape), pipeline_mode=pl.Buffered(1))
    cache = pl.BlockSpec((None, SB, KV_WIDTH, WINDOW), lambda i: (layer, i, 0, 0))
    state = pl.BlockSpec((None, POOL_STATE, SB, D_MODEL), lambda i: (layer, 0, i, 0))
    in_specs = [
        pl.BlockSpec(memory_space=pltpu.SMEM),
        full((R, D_MODEL)), full((R, PLE_DIM)), state, cache, cache,
        full((R, LANES)), full((R, LANES)), full((R, LANES)), full(bmain.shape), full(bnew.shape), full(sel.shape),
    ] + _weight_specs(layer)
    out_specs = [full((R, D_MODEL)), cache, cache, state]
    out_shape = [jax.ShapeDtypeStruct((R, D_MODEL), F32), jax.ShapeDtypeStruct(ck.shape, F32),
                 jax.ShapeDtypeStruct(cv.shape, F32), jax.ShapeDtypeStruct(st.shape, F32)]
    scratch = [pltpu.VMEM((R, D_MODEL), F32) for _ in range(4)] + [pltpu.VMEM((R, KV_WIDTH), F32) for _ in range(2)]
    args = [w["sinks2"][layer], h, p, st, ck, cv, tabs[0], tabs[1], tabs[2], bmain, bnew, sel] + _weight_args(w)
    aliases = {}
    if stacked is not None:
        aliases = {len(args) + k: 1 + k for k in range(len(stacked))}
        args += list(stacked)
        in_specs += [pl.BlockSpec(memory_space=pl.ANY)] * len(stacked)
    return pl.pallas_call(
        functools.partial(_sample_kernel, nb=nb, n_alias=len(aliases)),
        grid=(nb // SB,),
        in_specs=in_specs,
        out_specs=out_specs,
        out_shape=out_shape,
        scratch_shapes=scratch,
        input_output_aliases=aliases,
        compiler_params=pltpu.CompilerParams(dimension_semantics=("arbitrary",), vmem_limit_bytes=VMEM_LIMIT),
        name="sample_layer",
    )(*args)


def _rope_tables(pos):
    inv_freq = ROPE_THETA ** (-jnp.arange(0, ROPE_DIM, 2, dtype=F32) / ROPE_DIM)
    ang = pos.astype(F32)[:, None] * inv_freq[None, :]
    cos, sin = jnp.cos(ang), jnp.sin(ang)
    n = pos.shape[0]
    half = ROPE_DIM // 2
    c64 = jnp.concatenate([cos, cos, jnp.ones((n, HEAD_DIM - ROPE_DIM), F32)], axis=1)
    a64 = jnp.concatenate([-sin, jnp.zeros((n, HEAD_DIM - half), F32)], axis=1)
    b64 = jnp.concatenate([jnp.zeros((n, half), F32), sin, jnp.zeros((n, HEAD_DIM - ROPE_DIM), F32)], axis=1)
    rep = LANES // HEAD_DIM
    return tuple(jnp.tile(x, (1, rep)) for x in (c64, a64, b64))


def _prompt_bias():
    q = np.arange(WINDOW)[:, None]
    j = np.arange(WINDOW)[None, :]
    prev = np.where(j >= q, 0.0, NEG)
    cur = np.where(j <= q, 0.0, NEG)
    normal = np.concatenate([prev, cur], axis=1)
    first = np.concatenate([np.full_like(prev, NEG), cur], axis=1)
    return jnp.asarray(np.stack([normal, first]), F32)


def _sample_bias(dec):
    rq = np.arange(dec * SB)
    tq, bq = rq // SB, rq % SB
    cm = np.arange(SB * WINDOW)
    bm, jm = cm // WINDOW, cm % WINDOW
    main = np.where((bq[:, None] == bm[None, :]) & (jm[None, :] >= tq[:, None]), 0.0, NEG)
    new = np.where((bq[:, None] == bq[None, :]) & (tq[None, :] <= tq[:, None]), 0.0, NEG)
    sel = np.zeros((SB, dec * SB, WINDOW))
    for r in rq:
        sel[bq[r], r, WINDOW - dec + tq[r]] = 1.0
    return jnp.asarray(main, F32), jnp.asarray(new, F32), jnp.asarray(sel, BF16)


def _prep_weights(norm_g, w_in, q_norm_g, k_norm_g, sinks, pool_map, pool_scale,
                  w_proj_pool, w_proj_attn, w_out, w_ple, w_ple_gate):
    qscale = HEAD_DIM ** -0.5 * LOG2E
    blk = np.kron(np.eye(256 // HEAD_DIM), np.full((HEAD_DIM, HEAD_DIM), 1.0 / HEAD_DIM))
    return dict(
        sinks2=sinks * LOG2E,
        norm_g=norm_g[:, None, :],
        gq=jnp.tile(q_norm_g * qscale, (1, N_HEADS))[:, None, :],
        gk=jnp.tile(k_norm_g, (1, N_KV_HEADS))[:, None, :],
        pool_scale=pool_scale[:, None, :],
        w_in=w_in.astype(BF16),
        hn=jnp.asarray(blk, BF16),
        pool_map=pool_map.astype(BF16),
        w_pp=w_proj_pool.astype(BF16), w_pa=w_proj_attn.astype(BF16),
        w_out=w_out.astype(BF16), w_g=w_ple_gate.astype(BF16), w_ple=w_ple.astype(BF16),
    )


def kernel(x_prompt, x_sample, cache_k, cache_v, state_pool, p_prompt, p_sample, norm_g, w_in, q_norm_g, k_norm_g, sinks, pool_map, pool_scale, w_proj_pool, w_proj_attn, w_out, w_ple, w_ple_gate):
    depth = w_in.shape[0]
    B, T, _ = x_prompt.shape
    nb, dec, _ = x_sample.shape
    assert T % TQ == 0 and TQ % WINDOW == 0 and nb % SB == 0
    ptabs = _rope_tables(jnp.arange(T, dtype=jnp.int32))
    stabs = _rope_tables(jnp.repeat(PAST_LEN + jnp.arange(dec, dtype=jnp.int32), nb))
    pbias = _prompt_bias()
    bmain, bnew, sel = _sample_bias(dec)

    hp = x_prompt
    hs = x_sample.transpose(1, 0, 2).reshape(dec * nb, D_MODEL)
    ck_all = cache_k.transpose(0, 1, 3, 4, 2).reshape(depth, nb, KV_WIDTH, WINDOW)
    cv_all = cache_v.transpose(0, 1, 3, 4, 2).reshape(depth, nb, KV_WIDTH, WINDOW)
    st_all = state_pool.transpose(0, 2, 1, 3)
    stacked = None
    kp_l, vp_l, pp_l = [], [], []
    w = _prep_weights(norm_g, w_in, q_norm_g, k_norm_g, sinks, pool_map, pool_scale,
                      w_proj_pool, w_proj_attn, w_out, w_ple, w_ple_gate)
    for i in range(depth):
        hp, nk, nv, npool = _prompt_layer(hp, p_prompt, ptabs, pbias, w, i)
        kp_l.append(nk.reshape(B, WINDOW, N_KV_HEADS, HEAD_DIM))
        vp_l.append(nv.reshape(B, WINDOW, N_KV_HEADS, HEAD_DIM))
        pp_l.append(npool)

        ps_t = p_sample[i].transpose(1, 0, 2).reshape(dec * nb, PLE_DIM)
        hs, *stacked = _sample_layer(hs, ps_t, st_all, ck_all, cv_all, stabs, bmain, bnew, sel, w, i, stacked)
    uncache = lambda a: a.reshape(depth, nb, N_KV_HEADS, HEAD_DIM, WINDOW).transpose(0, 1, 4, 2, 3)
    new_k_sample, new_v_sample = uncache(stacked[0]), uncache(stacked[1])
    new_pool_sample = stacked[2].transpose(0, 2, 1, 3)
    y_sample = hs.reshape(dec, nb, D_MODEL).transpose(1, 0, 2)
    return (hp, y_sample, jnp.stack(kp_l), jnp.stack(vp_l), jnp.stack(pp_l),
            new_k_sample, new_v_sample, new_pool_sample)
```

```python
import functools
import math

import numpy as np
import jax
import jax.numpy as jnp
from jax import lax
from jax.experimental import pallas as pl
from jax.experimental.pallas import tpu as pltpu

D_MODEL = 1024
PLE_DIM = 256
POOL_WINDOWS = (2, 4, 8, 16)
POOL_GROUP = D_MODEL // len(POOL_WINDOWS)
POOL_STATE = max(POOL_WINDOWS) - 1
HEAD_DIM = 64
N_HEADS = D_MODEL // HEAD_DIM
N_KV_HEADS = 4
GQA_GROUP = N_HEADS // N_KV_HEADS
KV_WIDTH = N_KV_HEADS * HEAD_DIM
WINDOW = 128
ROPE_THETA = 500000.0
ROPE_DIM = HEAD_DIM // 4
EPS = 1e-6
PAST_LEN = 16384

C_U, C_ZP, C_Q, C_K, C_V, C_ZA, C_GA, C_GB, C_END = 0, 1024, 2048, 3072, 3328, 3584, 4608, 5632, 6656

LANES = 128
SUBLANES = 8
NEG = -1e30
LOG2E = math.log2(math.e)
VMEM_LIMIT = 60 * 1024 * 1024

TQ = 512
SB = 8

F32 = jnp.float32
BF16 = jnp.bfloat16


def _sigmoid(x):
    return 1.0 / (1.0 + jnp.exp2(x * (-LOG2E)))


def _dot(a, b):
    return jnp.dot(a, b, preferred_element_type=F32)


def _dot_t(a, b):
    return lax.dot_general(a, b, (((1,), (1,)), ((), ())), preferred_element_type=F32)


def _rope(x, cos, s1, s2):
    outs = []
    for c in range(x.shape[1] // LANES):
        xc = x[:, c * LANES:(c + 1) * LANES]
        outs.append(xc * cos + pltpu.roll(xc, LANES - ROPE_DIM // 2, 1) * s1 + pltpu.roll(xc, ROPE_DIM // 2, 1) * s2)
    return outs[0] if len(outs) == 1 else jnp.concatenate(outs, axis=1)


def _head_norm(x, hn, gain):
    sq = (x * x).astype(BF16)
    ms = jnp.concatenate([_dot(sq[:, c * 256:(c + 1) * 256], hn) for c in range(x.shape[1] // 256)], axis=1)
    return x * lax.rsqrt(ms + EPS) * gain


def _rms_in(h, g):
    ms = jnp.mean(h * h, axis=-1, keepdims=True)
    return (h * lax.rsqrt(ms + EPS) * g).astype(BF16)


def _tail(h, p, ya, yb, ga, gb, wpp, wpa, wout, wg, wple):
    m = _sigmoid(ga) * _dot(ya.astype(BF16), wpp) + _sigmoid(gb) * _dot(yb.astype(BF16), wpa)
    h1 = h + _dot(m.astype(BF16), wout)
    gate = _sigmoid(_dot(h1.astype(BF16), wg))
    return h1 + gate * _dot(p.astype(BF16), wple)


def _pool_map(r, pm_ref, ps):
    rb = r.astype(BF16)
    mapped = jnp.concatenate([_dot(rb[:, g * POOL_GROUP:(g + 1) * POOL_GROUP], pm_ref[g])
                              for g in range(len(POOL_WINDOWS))], axis=1)
    return mapped * ps


def _prompt_kernel(sinks_ref, h_ref, p_ref, cos_ref, s1_ref, s2_ref, bias_ref,
                   ng_ref, gq_ref, gk_ref, ps_ref, win_ref, hn_ref, pm_ref,
                   wpp_ref, wpa_ref, wout_ref, wg_ref, wple_ref,
                   y_ref, nk_ref, nv_ref, npool_ref,
                   kall, vall, u_s, s2_s, s4_s, s8_s, o_s):
    t = pl.program_id(1)
    nqb = TQ // WINDOW
    lane = lax.broadcasted_iota(jnp.int32, (WINDOW, LANES), 1)
    lo = lane < HEAD_DIM

    @pl.when(t == 0)
    def _():
        kall[...] = jnp.zeros_like(kall)
        vall[:, :, 0:LANES] = jnp.zeros((N_KV_HEADS, (nqb + 1) * 256, LANES), BF16)
        ones_pat = jnp.concatenate([jnp.where(lo, 1.0, 0.0), jnp.where(lo, 0.0, 1.0)], axis=0).astype(BF16)
        for kh in range(N_KV_HEADS):
            for x in range(nqb + 1):
                vall[kh, x * 256:(x + 1) * 256, LANES:2 * LANES] = ones_pat
        u_s[0:16, :] = jnp.zeros((16, D_MODEL), F32)
        s2_s[0:16, :] = jnp.zeros((16, D_MODEL), F32)
        s4_s[0:16, :] = jnp.zeros((16, 768), F32)
        s8_s[0:16, :] = jnp.zeros((16, 512), F32)

    @pl.when(t > 0)
    def _():
        for kh in range(N_KV_HEADS):
            kall[kh, 0:256, :] = kall[kh, nqb * 256:(nqb + 1) * 256, :]
            vall[kh, 0:256, 0:LANES] = vall[kh, nqb * 256:(nqb + 1) * 256, 0:LANES]

    h = h_ref[0]
    xn = _rms_in(h, ng_ref[...])

    def proj(c0, c1):
        return _dot(xn, win_ref[:, c0:c1])

    u = proj(C_U, C_ZP)
    u_s[16:16 + TQ, :] = u
    s2 = u + u_s[15:15 + TQ, :]
    s2_s[16:16 + TQ, :] = s2
    s4 = s2[:, 256:] + s2_s[14:14 + TQ, 256:]
    s4_s[16:16 + TQ, :] = s4
    s8 = s4[:, 256:] + s4_s[12:12 + TQ, 256:]
    s8_s[16:16 + TQ, :] = s8
    s16 = s8[:, 256:] + s8_s[8:8 + TQ, 256:]
    pos1 = (t * TQ + 1 + lax.broadcasted_iota(jnp.int32, (TQ, 1), 0)).astype(F32)
    wins = (s2[:, :256], s4[:, :256], s8[:, :256], s16)
    r = jnp.concatenate([wins[g] * (1.0 / jnp.minimum(pos1, float(w))) for g, w in enumerate(POOL_WINDOWS)],
                        axis=1) - u
    npool_ref[0] = u_s[TQ + 1:TQ + 16, :]
    u_s[0:16, :] = u_s[TQ:TQ + 16, :]
    s2_s[0:16, :] = s2_s[TQ:TQ + 16, :]
    s4_s[0:16, :] = s4_s[TQ:TQ + 16, :]
    s8_s[0:16, :] = s8_s[TQ:TQ + 16, :]
    zp = proj(C_ZP, C_Q)
    ya = _pool_map(r, pm_ref, ps_ref[...]) * (zp * _sigmoid(zp))

    cos, s1, s2t = cos_ref[...], s1_ref[...], s2_ref[...]
    hn = hn_ref[...]
    qb = _rope(_head_norm(proj(C_Q, C_K), hn, gq_ref[...]), cos, s1, s2t).astype(BF16)
    kr = _rope(_head_norm(proj(C_K, C_V), hn, gk_ref[...]), cos, s1, s2t)
    v = proj(C_V, C_ZA)
    nk_ref[0] = kr[TQ - WINDOW:, :]
    nv_ref[0] = v[TQ - WINDOW:, :]

    for x in range(nqb):
        rows = slice(x * WINDOW, (x + 1) * WINDOW)
        base = (x + 1) * 256
        for pr in range(N_KV_HEADS // 2):
            for src, dst, col in ((kr, kall, None), (v, vall, slice(0, LANES))):
                a = src[rows, pr * LANES:(pr + 1) * LANES]
                ra = pltpu.roll(a, HEAD_DIM, 1)
                parts = ((jnp.where(lo, a, 0.0), jnp.where(lo, 0.0, ra)),
                         (jnp.where(lo, ra, 0.0), jnp.where(lo, 0.0, a)))
                for e in range(2):
                    kh = 2 * pr + e
                    lh = jnp.concatenate(parts[e], axis=0).astype(BF16)
                    if col is None:
                        dst[kh, base:base + 256, :] = lh
                    else:
                        dst[kh, base:base + 256, col] = lh

    def tiled_bias(b):
        b4 = jnp.concatenate([b[:, :LANES], b[:, :LANES], b[:, LANES:], b[:, LANES:]], axis=1)
        return jnp.concatenate([b4, b4], axis=0)

    biases = [tiled_bias(jnp.where(t == 0, bias_ref[1], bias_ref[0]))] + [tiled_bias(bias_ref[0])] * (nqb - 1)
    for kh in range(N_KV_HEADS):
        for n in range(nqb):
            rows = slice(n * WINDOW, (n + 1) * WINDOW)
            qs = jnp.concatenate([qb[rows, kh * 256:kh * 256 + LANES],
                                  qb[rows, kh * 256 + LANES:(kh + 1) * 256]], axis=0)
            s = _dot_t(qs, kall[kh, n * 256:n * 256 + 512, :]) + biases[n]
            p_rows, sink_rows = [], []
            for rr in range(2):
                rs = slice(rr * WINDOW, (rr + 1) * WINDOW)
                pcols = [None] * 4
                sterm = []
                for e in range(2):
                    sink = sinks_ref[kh * GQA_GROUP + 2 * rr + e]
                    sp = s[rs, e * LANES:(e + 1) * LANES]
                    sc = s[rs, 256 + e * LANES:256 + (e + 1) * LANES]
                    mx = jnp.maximum(jnp.max(jnp.maximum(sp, sc), axis=-1, keepdims=True), sink)
                    pcols[e] = jnp.exp2(sp - mx)
                    pcols[2 + e] = jnp.exp2(sc - mx)
                    sterm.append(jnp.exp2(sink - mx))
                p_rows.append(jnp.concatenate(pcols, axis=1))
                sink_rows.append(jnp.where(lo, sterm[0], sterm[1]))
            pmat = jnp.concatenate(p_rows, axis=0).astype(BF16)
            o2 = _dot(pmat, vall[kh, n * 256:n * 256 + 512, :])
            o = o2[:, :LANES] / (o2[:, LANES:] + jnp.concatenate(sink_rows, axis=0))
            o_s[rows, kh * 256:kh * 256 + LANES] = o[:WINDOW]
            o_s[rows, kh * 256 + LANES:(kh + 1) * 256] = o[WINDOW:]

    za = proj(C_ZA, C_GA)
    yb = o_s[...] * (za * _sigmoid(za))
    y_ref[0] = _tail(h, p_ref[0], ya, yb, proj(C_GA, C_GB), proj(C_GB, C_END),
                     wpp_ref[...], wpa_ref[...], wout_ref[...], wg_ref[...], wple_ref[...])


def _const_spec(shape):
    nd = len(shape)
    return pl.BlockSpec(shape, lambda *_: (0,) * nd, pipeline_mode=pl.Buffered(1))


def _layer_spec(shape, layer):
    nd = len(shape)
    return pl.BlockSpec((None,) + tuple(shape), lambda *_: (layer,) + (0,) * nd, pipeline_mode=pl.Buffered(1))


def _weight_specs(layer):
    ls = lambda *shape: _layer_spec(shape, layer)
    return [ls(1, D_MODEL), ls(1, D_MODEL), ls(1, KV_WIDTH), ls(1, D_MODEL),
            ls(D_MODEL, C_END), _const_spec((256, 256)), ls(4, POOL_GROUP, POOL_GROUP),
            ls(D_MODEL, D_MODEL), ls(D_MODEL, D_MODEL), ls(D_MODEL, D_MODEL), ls(D_MODEL, D_MODEL), ls(PLE_DIM, D_MODEL)]


def _weight_args(w):
    return [w["norm_g"], w["gq"], w["gk"], w["pool_scale"], w["w_in"], w["hn"], w["pool_map"],
            w["w_pp"], w["w_pa"], w["w_out"], w["w_g"], w["w_ple"]]


def _prompt_layer(h, p, tabs, bias, w, layer):
    B, T, _ = h.shape
    nt = T // TQ
    nqb = TQ // WINDOW
    row = lambda width: pl.BlockSpec((1, TQ, width), lambda b, t: (b, t, 0))
    tab = pl.BlockSpec((TQ, LANES), lambda b, t: (t, 0))
    in_specs = [
        pl.BlockSpec(memory_space=pltpu.SMEM),
        row(D_MODEL), pl.BlockSpec((None, 1, TQ, PLE_DIM), lambda b, t: (layer, b, t, 0)), tab, tab, tab,
        _const_spec(bias.shape),
    ] + _weight_specs(layer)
    out_specs = [
        row(D_MODEL),
        pl.BlockSpec((1, WINDOW, KV_WIDTH), lambda b, t: (b, 0, 0)),
        pl.BlockSpec((1, WINDOW, KV_WIDTH), lambda b, t: (b, 0, 0)),
        pl.BlockSpec((1, POOL_STATE, D_MODEL), lambda b, t: (b, 0, 0)),
    ]
    out_shape = [
        jax.ShapeDtypeStruct((B, T, D_MODEL), F32),
        jax.ShapeDtypeStruct((B, WINDOW, KV_WIDTH), F32),
        jax.ShapeDtypeStruct((B, WINDOW, KV_WIDTH), F32),
        jax.ShapeDtypeStruct((B, POOL_STATE, D_MODEL), F32),
    ]
    scratch = [
        pltpu.VMEM((N_KV_HEADS, (nqb + 1) * 256, LANES), BF16),
        pltpu.VMEM((N_KV_HEADS, (nqb + 1) * 256, 2 * LANES), BF16),
        pltpu.VMEM((TQ + 16, D_MODEL), F32),
        pltpu.VMEM((TQ + 16, D_MODEL), F32),
        pltpu.VMEM((TQ + 16, 768), F32),
        pltpu.VMEM((TQ + 16, 512), F32),
        pltpu.VMEM((TQ, D_MODEL), F32),
    ]
    return pl.pallas_call(
        _prompt_kernel,
        grid=(B, nt),
        in_specs=in_specs,
        out_specs=out_specs,
        out_shape=out_shape,
        scratch_shapes=scratch,
        compiler_params=pltpu.CompilerParams(dimension_semantics=("arbitrary", "arbitrary"),
                                             vmem_limit_bytes=VMEM_LIMIT),
        name="prompt_layer",
    )(w["sinks2"][layer], h, p, tabs[0], tabs[1], tabs[2], bias, *_weight_args(w))


def _split3(x):
    hi = x.astype(BF16)
    r1 = x - hi.astype(F32)
    mid = r1.astype(BF16)
    return hi, mid, (r1 - mid.astype(F32)).astype(BF16)


def _sample_kernel(sinks_ref, h_ref, p_ref, st_ref, ck_ref, cv_ref, cos_ref, s1_ref, s2_ref, bmain_ref, bnew_ref, sel_ref,
                   ng_ref, gq_ref, gk_ref, ps_ref, win_ref, hn_ref, pm_ref,
                   wpp_ref, wpa_ref, wout_ref, wg_ref, wple_ref, *rest, nb, n_alias):
    y_ref, nkc_ref, nvc_ref, npool_ref, q_s, o_s, ya_s, nu_ref, nk_ref, nv_ref = rest[n_alias:]
    i = pl.program_id(0)
    nsteps = pl.num_programs(0)
    dec = h_ref.shape[0] // nb

    @pl.when(i == 0)
    def _():
        h = h_ref[...]
        xn = _rms_in(h, ng_ref[...])

        def proj(c0, c1):
            return _dot(xn, win_ref[:, c0:c1])

        nu_ref[...] = proj(C_U, C_ZP)
        zp = proj(C_ZP, C_Q)
        ya_s[...] = zp * _sigmoid(zp)

        cos, s1, s2t = cos_ref[...], s1_ref[...], s2_ref[...]
        hn = hn_ref[...]
        q_s[...] = _rope(_head_norm(proj(C_Q, C_K), hn, gq_ref[...]), cos, s1, s2t)
        nk_ref[...] = _rope(_head_norm(proj(C_K, C_V), hn, gk_ref[...]), cos, s1, s2t)
        nv_ref[...] = proj(C_V, C_ZA)

    b0 = pl.multiple_of(i * SB, SB)
    rpp = dec * SB
    nrow = N_HEADS * rpp
    lo = lax.broadcasted_iota(jnp.int32, (rpp, LANES), 1) < HEAD_DIM
    zero = jnp.zeros((rpp, LANES), F32)

    def rows_tb(ref, c0, c1):
        return jnp.concatenate([ref[pl.ds(pl.multiple_of(tt * nb + b0, SB), SB), c0:c1] for tt in range(dec)], axis=0)

    u_tb = rows_tb(nu_ref, 0, D_MODEL)
    seq = [st_ref[j] for j in range(POOL_STATE)] + [u_tb[tt * SB:(tt + 1) * SB] for tt in range(dec)]
    r_rows = []
    for tt in range(dec):
        e = POOL_STATE + tt
        cols = []
        for g, w in enumerate(POOL_WINDOWS):
            cs = slice(g * POOL_GROUP, (g + 1) * POOL_GROUP)
            acc = seq[e][:, cs]
            for d in range(1, w):
                acc = acc + seq[e - d][:, cs]
            cols.append(acc * (1.0 / w))
        r_rows.append(jnp.concatenate(cols, axis=1) - seq[e])
    ya_tb = _pool_map(jnp.concatenate(r_rows, axis=0), pm_ref, ps_ref[...])
    for tt in range(dec):
        rs = pl.ds(pl.multiple_of(tt * nb + b0, SB), SB)
        ya_s[rs, :] = ya_s[rs, :] * ya_tb[tt * SB:(tt + 1) * SB]

    pieces = []
    for kh in range(N_KV_HEADS):
        for gp in range(GQA_GROUP // 2):
            c = kh * 2 + gp
            x = rows_tb(q_s, c * LANES, (c + 1) * LANES)
            rx = pltpu.roll(x, HEAD_DIM, 1)
            for e in range(2):
                src = x if e == kh % 2 else rx
                half = jnp.where(lo, src, 0.0) if kh % 2 == 0 else jnp.where(lo, 0.0, src)
                pieces.append(jnp.concatenate([half, zero] if kh < 2 else [zero, half], axis=1))
    lhs = jnp.concatenate(pieces, axis=0).astype(BF16)
    kmain = jnp.concatenate([ck_ref[bl].astype(BF16) for bl in range(SB)], axis=1)
    vmain = jnp.concatenate([cv_ref[bl].astype(BF16) for bl in range(SB)], axis=1)
    knew_f = rows_tb(nk_ref, 0, KV_WIDTH)
    vnew_f = rows_tb(nv_ref, 0, KV_WIDTH)
    knew = knew_f.astype(BF16)
    vnew = vnew_f.astype(BF16)
    s_main = (_dot(lhs, kmain).reshape(N_HEADS, rpp, SB * WINDOW) + bmain_ref[...][None]).reshape(nrow, SB * WINDOW)
    s_new = (_dot_t(lhs, knew).reshape(N_HEADS, rpp, rpp) + bnew_ref[...][None]).reshape(nrow, rpp)
    sink = jnp.concatenate([jnp.full((rpp, 1), sinks_ref[hh], F32) for hh in range(N_HEADS)], axis=0)
    mx = jnp.maximum(jnp.maximum(jnp.max(s_main, axis=-1, keepdims=True), jnp.max(s_new, axis=-1, keepdims=True)), sink)
    p_main = jnp.exp2(s_main - mx).astype(BF16)
    p_new = jnp.exp2(s_new - mx).astype(BF16)
    den = (jnp.sum(p_main.astype(F32), axis=-1, keepdims=True) + jnp.sum(p_new.astype(F32), axis=-1, keepdims=True)
           + jnp.exp2(sink - mx))
    o = (_dot_t(p_main, vmain) + _dot(p_new, vnew)) / den
    for kh in range(N_KV_HEADS):
        for gp in range(GQA_GROUP // 2):
            halves = []
            for e in range(2):
                hh = kh * GQA_GROUP + 2 * gp + e
                x = o[hh * rpp:(hh + 1) * rpp, (kh // 2) * LANES:(kh // 2 + 1) * LANES]
                halves.append(x if e == kh % 2 else pltpu.roll(x, HEAD_DIM, 1))
            dest = jnp.where(lo, halves[0], halves[1])
            c = kh * 2 + gp
            for tt in range(dec):
                o_s[pl.ds(pl.multiple_of(tt * nb + b0, SB), SB), c * LANES:(c + 1) * LANES] = dest[tt * SB:(tt + 1) * SB]

    slot = lax.broadcasted_iota(jnp.int32, (KV_WIDTH, WINDOW), 1)
    for src_ref, new_f, dst_ref in ((ck_ref, knew_f, nkc_ref), (cv_ref, vnew_f, nvc_ref)):
        pieces3 = _split3(new_f)
        for bl in range(SB):
            cols = None
            for pc in pieces3:
                part = lax.dot_general(pc, sel_ref[bl], (((0,), (0,)), ((), ())), preferred_element_type=F32)
                cols = part if cols is None else cols + part
            dst_ref[bl] = jnp.where(slot < WINDOW - dec, pltpu.roll(src_ref[bl], WINDOW - dec, 1), cols)

    for j in range(POOL_STATE):
        npool_ref[j] = seq[j + dec]

    @pl.when(i == nsteps - 1)
    def _():
        h = h_ref[...]
        xn = _rms_in(h, ng_ref[...])
        za = _dot(xn, win_ref[:, C_ZA:C_GA])
        yb = o_s[...] * (za * _sigmoid(za))
        y_ref[...] = _tail(h, p_ref[...], ya_s[...], yb, _dot(xn, win_ref[:, C_GA:C_GB]), _dot(xn, win_ref[:, C_GB:C_END]),
                           wpp_ref[...], wpa_ref[...], wout_ref[...], wg_ref[...], wple_ref[...])


def _sample_layer(h, p, st, ck, cv, tabs, bmain, bnew, sel, w, layer, stacked):
    R = h.shape[0]
    nb = ck.shape[1]
    full = lambda shape: pl.BlockSpec(shape, lambda i: (0,) * len(shape), pipeline_mode=pl.Buffered(1))
    cache = pl.BlockSpec((None, SB, KV_WIDTH, WINDOW), lambda i: (layer, i, 0, 0))
    state = pl.BlockSpec((None, POOL_STATE, SB, D_MODEL), lambda i: (layer, 0, i, 0))
    in_specs = [
        pl.BlockSpec(memory_space=pltpu.SMEM),
        full((R, D_MODEL)), full((R, PLE_DIM)), state, cache, cache,
        full((R, LANES)), full((R, LANES)), full((R, LANES)), full(bmain.shape), full(bnew.shape), full(sel.shape),
    ] + _weight_specs(layer)
    out_specs = [full((R, D_MODEL)), cache, cache, state]
    out_shape = [jax.ShapeDtypeStruct((R, D_MODEL), F32), jax.ShapeDtypeStruct(ck.shape, F32),
                 jax.ShapeDtypeStruct(cv.shape, F32), jax.ShapeDtypeStruct(st.shape, F32)]
    scratch = [pltpu.VMEM((R, D_MODEL), F32) for _ in range(4)] + [pltpu.VMEM((R, KV_WIDTH), F32) for _ in range(2)]
    args = [w["sinks2"][layer], h, p, st, ck, cv, tabs[0], tabs[1], tabs[2], bmain, bnew, sel] + _weight_args(w)
    aliases = {}
    if stacked is not None:
        aliases = {len(args) + k: 1 + k for k in range(len(stacked))}
        args += list(stacked)
        in_specs += [pl.BlockSpec(memory_space=pl.ANY)] * len(stacked)
    return pl.pallas_call(
        functools.partial(_sample_kernel, nb=nb, n_alias=len(aliases)),
        grid=(nb // SB,),
        in_specs=in_specs,
        out_specs=out_specs,
        out_shape=out_shape,
        scratch_shapes=scratch,
        input_output_aliases=aliases,
        compiler_params=pltpu.CompilerParams(dimension_semantics=("arbitrary",), vmem_limit_bytes=VMEM_LIMIT),
        name="sample_layer",
    )(*args)


def _rope_tables(pos):
    inv_freq = ROPE_THETA ** (-jnp.arange(0, ROPE_DIM, 2, dtype=F32) / ROPE_DIM)
    ang = pos.astype(F32)[:, None] * inv_freq[None, :]
    cos, sin = jnp.cos(ang), jnp.sin(ang)
    n = pos.shape[0]
    half = ROPE_DIM // 2
    c64 = jnp.concatenate([cos, cos, jnp.ones((n, HEAD_DIM - ROPE_DIM), F32)], axis=1)
    a64 = jnp.concatenate([-sin, jnp.zeros((n, HEAD_DIM - half), F32)], axis=1)
    b64 = jnp.concatenate([jnp.zeros((n, half), F32), sin, jnp.zeros((n, HEAD_DIM - ROPE_DIM), F32)], axis=1)
    rep = LANES // HEAD_DIM
    return tuple(jnp.tile(x, (1, rep)) for x in (c64, a64, b64))


def _prompt_bias():
    q = np.arange(WINDOW)[:, None]
    j = np.arange(WINDOW)[None, :]
    prev = np.where(j >= q, 0.0, NEG)
    cur = np.where(j <= q, 0.0, NEG)
    normal = np.concatenate([prev, cur], axis=1)
    first = np.concatenate([np.full_like(prev, NEG), cur], axis=1)
    return jnp.asarray(np.stack([normal, first]), F32)


def _sample_bias(dec):
    rq = np.arange(dec * SB)
    tq, bq = rq // SB, rq % SB
    cm = np.arange(SB * WINDOW)
    bm, jm = cm // WINDOW, cm % WINDOW
    main = np.where((bq[:, None] == bm[None, :]) & (jm[None, :] >= tq[:, None]), 0.0, NEG)
    new = np.where((bq[:, None] == bq[None, :]) & (tq[None, :] <= tq[:, None]), 0.0, NEG)
    sel = np.zeros((SB, dec * SB, WINDOW))
    for r in rq:
        sel[bq[r], r, WINDOW - dec + tq[r]] = 1.0
    return jnp.asarray(main, F32), jnp.asarray(new, F32), jnp.asarray(sel, BF16)


def _prep_weights(norm_g, w_in, q_norm_g, k_norm_g, sinks, pool_map, pool_scale,
                  w_proj_pool, w_proj_attn, w_out, w_ple, w_ple_gate):
    qscale = HEAD_DIM ** -0.5 * LOG2E
    blk = np.kron(np.eye(256 // HEAD_DIM), np.full((HEAD_DIM, HEAD_DIM), 1.0 / HEAD_DIM))
    return dict(
        sinks2=sinks * LOG2E,
        norm_g=norm_g[:, None, :],
        gq=jnp.tile(q_norm_g * qscale, (1, N_HEADS))[:, None, :],
        gk=jnp.tile(k_norm_g, (1, N_KV_HEADS))[:, None, :],
        pool_scale=pool_scale[:, None, :],
        w_in=w_in.astype(BF16),
        hn=jnp.asarray(blk, BF16),
        pool_map=pool_map.astype(BF16),
        w_pp=w_proj_pool.astype(BF16), w_pa=w_proj_attn.astype(BF16),
        w_out=w_out.astype(BF16), w_g=w_ple_gate.astype(BF16), w_ple=w_ple.astype(BF16),
    )


def kernel(x_prompt, x_sample, cache_k, cache_v, state_pool, p_prompt, p_sample, norm_g, w_in, q_norm_g, k_norm_g, sinks, pool_map, pool_scale, w_proj_pool, w_proj_attn, w_out, w_ple, w_ple_gate):
    depth = w_in.shape[0]
    B, T, _ = x_prompt.shape
    nb, dec, _ = x_sample.shape
    assert T % TQ == 0 and TQ % WINDOW == 0 and nb % SB == 0
    ptabs = _rope_tables(jnp.arange(T, dtype=jnp.int32))
    stabs = _rope_tables(jnp.repeat(PAST_LEN + jnp.arange(dec, dtype=jnp.int32), nb))
    pbias = _prompt_bias()
    bmain, bnew, sel = _sample_bias(dec)

    hp = x_prompt
    hs = x_sample.transpose(1, 0, 2).reshape(dec * nb, D_MODEL)
    ck_all = cache_k.transpose(0, 1, 3, 4, 2).reshape(depth, nb, KV_WIDTH, WINDOW)
    cv_all = cache_v.transpose(0, 1, 3, 4, 2).reshape(depth, nb, KV_WIDTH, WINDOW)
    st_all = state_pool.transpose(0, 2, 1, 3)
    stacked = None
    kp_l, vp_l, pp_l = [], [], []
    w = _prep_weights(norm_g, w_in, q_norm_g, k_norm_g, sinks, pool_map, pool_scale,
                      w_proj_pool, w_proj_attn, w_out, w_ple, w_ple_gate)
    for i in range(depth):
        hp, nk, nv, npool = _prompt_layer(hp, p_prompt, ptabs, pbias, w, i)
        kp_l.append(nk.reshape(B, WINDOW, N_KV_HEADS, HEAD_DIM))
        vp_l.append(nv.reshape(B, WINDOW, N_KV_HEADS, HEAD_DIM))
        pp_l.append(npool)

        ps_t = p_sample[i].transpose(1, 0, 2).reshape(dec * nb, PLE_DIM)
        hs, *stacked = _sample_layer(hs, ps_t, st_all, ck_all, cv_all, stabs, bmain, bnew, sel, w, i, stacked)
    uncache = lambda a: a.reshape(depth, nb, N_KV_HEADS, HEAD_DIM, WINDOW).transpose(0, 1, 4, 2, 3)
    new_k_sample, new_v_sample = uncache(stacked[0]), uncache(stacked[1])
    new_pool_sample = stacked[2].transpose(0, 2, 1, 3)
    y_sample = hs.reshape(dec, nb, D_MODEL).transpose(1, 0, 2)
    return (hp, y_sample, jnp.stack(kp_l), jnp.stack(vp_l), jnp.stack(pp_l),
            new_k_sample, new_v_sample, new_pool_sample)
```

```python
import functools
import math

import numpy as np
import jax
import jax.numpy as jnp
from jax import lax
from jax.experimental import pallas as pl
from jax.experimental.pallas import tpu as pltpu

D_MODEL = 1024
PLE_DIM = 256
POOL_WINDOWS = (2, 4, 8, 16)
POOL_GROUP = D_MODEL // len(POOL_WINDOWS)
POOL_STATE = max(POOL_WINDOWS) - 1
HEAD_DIM = 64
N_HEADS = D_MODEL // HEAD_DIM
N_KV_HEADS = 4
GQA_GROUP = N_HEADS // N_KV_HEADS
KV_WIDTH = N_KV_HEADS * HEAD_DIM
WINDOW = 128
ROPE_THETA = 500000.0
ROPE_DIM = HEAD_DIM // 4
EPS = 1e-6
PAST_LEN = 16384

C_U, C_ZP, C_Q, C_K, C_V, C_ZA, C_GA, C_GB, C_END = 0, 1024, 2048, 3072, 3328, 3584, 4608, 5632, 6656

LANES = 128
SUBLANES = 8
NEG = -1e30
LOG2E = math.log2(math.e)
VMEM_LIMIT = 60 * 1024 * 1024

TQ = 512
SB = 8

F32 = jnp.float32
BF16 = jnp.bfloat16


def _sigmoid(x):
    return 1.0 / (1.0 + jnp.exp2(x * (-LOG2E)))


def _dot(a, b):
    return jnp.dot(a, b, preferred_element_type=F32)


def _dot_t(a, b):
    return lax.dot_general(a, b, (((1,), (1,)), ((), ())), preferred_element_type=F32)


def _rope(x, cos, s1, s2):
    outs = []
    for c in range(x.shape[1] // LANES):
        xc = x[:, c * LANES:(c + 1) * LANES]
        outs.append(xc * cos + pltpu.roll(xc, LANES - ROPE_DIM // 2, 1) * s1 + pltpu.roll(xc, ROPE_DIM // 2, 1) * s2)
    return outs[0] if len(outs) == 1 else jnp.concatenate(outs, axis=1)


def _head_norm(x, hn, gain):
    sq = (x * x).astype(BF16)
    ms = jnp.concatenate([_dot(sq[:, c * 256:(c + 1) * 256], hn) for c in range(x.shape[1] // 256)], axis=1)
    return x * lax.rsqrt(ms + EPS) * gain


def _rms_in(h, g):
    ms = jnp.mean(h * h, axis=-1, keepdims=True)
    return (h * lax.rsqrt(ms + EPS) * g).astype(BF16)


def _tail(h, p, ya, yb, ga, gb, wpp, wpa, wout, wg, wple):
    m = _sigmoid(ga) * _dot(ya.astype(BF16), wpp) + _sigmoid(gb) * _dot(yb.astype(BF16), wpa)
    h1 = h + _dot(m.astype(BF16), wout)
    gate = _sigmoid(_dot(h1.astype(BF16), wg))
    return h1 + gate * _dot(p.astype(BF16), wple)


def _pool_map(r, pm_ref, ps):
    rb = r.astype(BF16)
    mapped = jnp.concatenate([_dot(rb[:, g * POOL_GROUP:(g + 1) * POOL_GROUP], pm_ref[g])
                              for g in range(len(POOL_WINDOWS))], axis=1)
    return mapped * ps


def _prompt_kernel(sinks_ref, h_ref, p_ref, cos_ref, s1_ref, s2_ref, bias_ref,
                   ng_ref, gq_ref, gk_ref, ps_ref, win_ref, hn_ref, pm_ref,
                   wpp_ref, wpa_ref, wout_ref, wg_ref, wple_ref,
                   y_ref, nk_ref, nv_ref, npool_ref,
                   kall, vall, u_s, s2_s, s4_s, s8_s, o_s):
    t = pl.program_id(1)
    nqb = TQ // WINDOW
    lane = lax.broadcasted_iota(jnp.int32, (WINDOW, LANES), 1)
    lo = lane < HEAD_DIM

    @pl.when(t == 0)
    def _():
        kall[...] = jnp.zeros_like(kall)
        vall[:, :, 0:LANES] = jnp.zeros((N_KV_HEADS, (nqb + 1) * 256, LANES), BF16)
        ones_pat = jnp.concatenate([jnp.where(lo, 1.0, 0.0), jnp.where(lo, 0.0, 1.0)], axis=0).astype(BF16)
        for kh in range(N_KV_HEADS):
            for x in range(nqb + 1):
                vall[kh, x * 256:(x + 1) * 256, LANES:2 * LANES] = ones_pat
        u_s[0:16, :] = jnp.zeros((16, D_MODEL), F32)
        s2_s[0:16, :] = jnp.zeros((16, D_MODEL), F32)
        s4_s[0:16, :] = jnp.zeros((16, 768), F32)
        s8_s[0:16, :] = jnp.zeros((16, 512), F32)

    @pl.when(t > 0)
    def _():
        for kh in range(N_KV_HEADS):
            kall[kh, 0:256, :] = kall[kh, nqb * 256:(nqb + 1) * 256, :]
            vall[kh, 0:256, 0:LANES] = vall[kh, nqb * 256:(nqb + 1) * 256, 0:LANES]

    h = h_ref[0]
    xn = _rms_in(h, ng_ref[...])

    def proj(c0, c1):
        return _dot(xn, win_ref[:, c0:c1])

    res = {}

    def fillers():
        parts = []
        for c in range(D_MODEL // 256):
            parts.append(proj(C_U + c * 256, C_U + (c + 1) * 256))
            yield
        u = jnp.concatenate(parts, axis=1)
        u_s[16:16 + TQ, :] = u
        s2 = u + u_s[15:15 + TQ, :]
        s2_s[16:16 + TQ, :] = s2
        s4 = s2[:, 256:] + s2_s[14:14 + TQ, 256:]
        s4_s[16:16 + TQ, :] = s4
        s8 = s4[:, 256:] + s4_s[12:12 + TQ, 256:]
        s8_s[16:16 + TQ, :] = s8
        s16 = s8[:, 256:] + s8_s[8:8 + TQ, 256:]
        pos1 = (t * TQ + 1 + lax.broadcasted_iota(jnp.int32, (TQ, 1), 0)).astype(F32)
        wins = (s2[:, :256], s4[:, :256], s8[:, :256], s16)
        r = jnp.concatenate([wins[g] * (1.0 / jnp.minimum(pos1, float(w))) for g, w in enumerate(POOL_WINDOWS)],
                            axis=1) - u
        npool_ref[0] = u_s[TQ + 1:TQ + 16, :]
        u_s[0:16, :] = u_s[TQ:TQ + 16, :]
        s2_s[0:16, :] = s2_s[TQ:TQ + 16, :]
        s4_s[0:16, :] = s4_s[TQ:TQ + 16, :]
        s8_s[0:16, :] = s8_s[TQ:TQ + 16, :]
        mapped = _pool_map(r, pm_ref, ps_ref[...])
        yield
        for name, c_lo in (("zp", C_ZP), ("za", C_ZA), ("ga", C_GA), ("gb", C_GB)):
            parts = []
            for c in range(D_MODEL // 256):
                parts.append(proj(c_lo + c * 256, c_lo + (c + 1) * 256))
                yield
            res[name] = jnp.concatenate(parts, axis=1)
        yab = (mapped * (res["zp"] * _sigmoid(res["zp"]))).astype(BF16)
        pb = p_ref[0].astype(BF16)
        for name, lhs, w_ref in (("pp", yab, wpp_ref), ("ple", pb, wple_ref)):
            parts = []
            for c in range(D_MODEL // 256):
                parts.append(_dot(lhs, w_ref[:, c * 256:(c + 1) * 256]))
                yield
            res[name] = jnp.concatenate(parts, axis=1)

    def tiled_bias(b):
        b4 = jnp.concatenate([b[:, :LANES], b[:, :LANES], b[:, LANES:], b[:, LANES:]], axis=1)
        return jnp.concatenate([b4, b4], axis=0)

    biases = [tiled_bias(jnp.where(t == 0, bias_ref[1], bias_ref[0]))] + [tiled_bias(bias_ref[0])] * (nqb - 1)
    chains = [(kh, n) for kh in range(N_KV_HEADS) for n in range(nqb)]
    n_fill = 7 * (D_MODEL // 256) + 1
    fill = fillers()

    def scores(kh, n):
        rows = slice(n * WINDOW, (n + 1) * WINDOW)
        qs = jnp.concatenate([qb[rows, kh * 256:kh * 256 + LANES],
                              qb[rows, kh * 256 + LANES:(kh + 1) * 256]], axis=0)
        return _dot_t(qs, kall[kh, n * 256:n * 256 + 512, :]) + biases[n]

    q_raw, k_raw, v = proj(C_Q, C_K), proj(C_K, C_V), proj(C_V, C_ZA)
    pre = D_MODEL // 256 + 1
    for _ in range(pre):
        next(fill)
    done = pre
    cos, s1, s2t = cos_ref[...], s1_ref[...], s2_ref[...]
    hn = hn_ref[...]
    kr = _rope(_head_norm(k_raw, hn, gk_ref[...]), cos, s1, s2t)
    qb = _rope(_head_norm(q_raw, hn, gq_ref[...]), cos, s1, s2t).astype(BF16)
    nk_ref[0] = kr[TQ - WINDOW:, :]
    nv_ref[0] = v[TQ - WINDOW:, :]
    for x in range(nqb):
        rows = slice(x * WINDOW, (x + 1) * WINDOW)
        base = (x + 1) * 256
        for pr in range(N_KV_HEADS // 2):
            for src, dst, col in ((kr, kall, None), (v, vall, slice(0, LANES))):
                a = src[rows, pr * LANES:(pr + 1) * LANES]
                ra = pltpu.roll(a, HEAD_DIM, 1)
                parts = ((jnp.where(lo, a, 0.0), jnp.where(lo, 0.0, ra)),
                         (jnp.where(lo, ra, 0.0), jnp.where(lo, 0.0, a)))
                for e in range(2):
                    kh = 2 * pr + e
                    lh = jnp.concatenate(parts[e], axis=0).astype(BF16)
                    if col is None:
                        dst[kh, base:base + 256, :] = lh
                    else:
                        dst[kh, base:base + 256, col] = lh

    s_next = scores(*chains[0])
    for ci, (kh, n) in enumerate(chains):
        s = s_next
        if ci + 1 < len(chains):
            s_next = scores(*chains[ci + 1])
        want = pre + ((n_fill - pre) * (ci + 1)) // len(chains)
        while done < want:
            next(fill)
            done += 1
        rows = slice(n * WINDOW, (n + 1) * WINDOW)
        p_rows, sink_rows = [], []
        for rr in range(2):
            rs = slice(rr * WINDOW, (rr + 1) * WINDOW)
            pcols = [None] * 4
            sterm = []
            for e in range(2):
                sink = sinks_ref[kh * GQA_GROUP + 2 * rr + e]
                sp = s[rs, e * LANES:(e + 1) * LANES]
                sc = s[rs, 256 + e * LANES:256 + (e + 1) * LANES]
                mx = jnp.maximum(jnp.max(jnp.maximum(sp, sc), axis=-1, keepdims=True), sink)
                pcols[e] = jnp.exp2(sp - mx)
                pcols[2 + e] = jnp.exp2(sc - mx)
                sterm.append(jnp.exp2(sink - mx))
            p_rows.append(jnp.concatenate(pcols, axis=1))
            sink_rows.append(jnp.where(lo, sterm[0], sterm[1]))
        pmat = jnp.concatenate(p_rows, axis=0).astype(BF16)
        o2 = _dot(pmat, vall[kh, n * 256:n * 256 + 512, :])
        o = o2[:, :LANES] / (o2[:, LANES:] + jnp.concatenate(sink_rows, axis=0))
        o_s[rows, kh * 256:kh * 256 + LANES] = o[:WINDOW]
        o_s[rows, kh * 256 + LANES:(kh + 1) * 256] = o[WINDOW:]
    for _ in fill:
        pass

    za, ga, gb = res["za"], res["ga"], res["gb"]
    yb = o_s[...] * (za * _sigmoid(za))
    m = _sigmoid(ga) * res["pp"] + _sigmoid(gb) * _dot(yb.astype(BF16), wpa_ref[...])
    h1 = h + _dot(m.astype(BF16), wout_ref[...])
    gate = _sigmoid(_dot(h1.astype(BF16), wg_ref[...]))
    y_ref[0] = h1 + gate * res["ple"]


def _const_spec(shape):
    nd = len(shape)
    return pl.BlockSpec(shape, lambda *_: (0,) * nd, pipeline_mode=pl.Buffered(1))


def _layer_spec(shape, layer):
    nd = len(shape)
    return pl.BlockSpec((None,) + tuple(shape), lambda *_: (layer,) + (0,) * nd, pipeline_mode=pl.Buffered(1))


def _weight_specs(layer):
    ls = lambda *shape: _layer_spec(shape, layer)
    return [ls(1, D_MODEL), ls(1, D_MODEL), ls(1, KV_WIDTH), ls(1, D_MODEL),
            ls(D_MODEL, C_END), _const_spec((256, 256)), ls(4, POOL_GROUP, POOL_GROUP),
            ls(D_MODEL, D_MODEL), ls(D_MODEL, D_MODEL), ls(D_MODEL, D_MODEL), ls(D_MODEL, D_MODEL), ls(PLE_DIM, D_MODEL)]


def _weight_args(w):
    return [w["norm_g"], w["gq"], w["gk"], w["pool_scale"], w["w_in"], w["hn"], w["pool_map"],
            w["w_pp"], w["w_pa"], w["w_out"], w["w_g"], w["w_ple"]]


def _prompt_layer(h, p, tabs, bias, w, layer):
    B, T, _ = h.shape
    nt = T // TQ
    nqb = TQ // WINDOW
    row = lambda width: pl.BlockSpec((1, TQ, width), lambda b, t: (b, t, 0))
    tab = pl.BlockSpec((TQ, LANES), lambda b, t: (t, 0))
    in_specs = [
        pl.BlockSpec(memory_space=pltpu.SMEM),
        row(D_MODEL), pl.BlockSpec((None, 1, TQ, PLE_DIM), lambda b, t: (layer, b, t, 0)), tab, tab, tab,
        _const_spec(bias.shape),
    ] + _weight_specs(layer)
    out_specs = [
        row(D_MODEL),
        pl.BlockSpec((1, WINDOW, KV_WIDTH), lambda b, t: (b, 0, 0)),
        pl.BlockSpec((1, WINDOW, KV_WIDTH), lambda b, t: (b, 0, 0)),
        pl.BlockSpec((1, POOL_STATE, D_MODEL), lambda b, t: (b, 0, 0)),
    ]
    out_shape = [
        jax.ShapeDtypeStruct((B, T, D_MODEL), F32),
        jax.ShapeDtypeStruct((B, WINDOW, KV_WIDTH), F32),
        jax.ShapeDtypeStruct((B, WINDOW, KV_WIDTH), F32),
        jax.ShapeDtypeStruct((B, POOL_STATE, D_MODEL), F32),
    ]
    scratch = [
        pltpu.VMEM((N_KV_HEADS, (nqb + 1) * 256, LANES), BF16),
        pltpu.VMEM((N_KV_HEADS, (nqb + 1) * 256, 2 * LANES), BF16),
        pltpu.VMEM((TQ + 16, D_MODEL), F32),
        pltpu.VMEM((TQ + 16, D_MODEL), F32),
        pltpu.VMEM((TQ + 16, 768), F32),
        pltpu.VMEM((TQ + 16, 512), F32),
        pltpu.VMEM((TQ, D_MODEL), F32),
    ]
    return pl.pallas_call(
        _prompt_kernel,
        grid=(B, nt),
        in_specs=in_specs,
        out_specs=out_specs,
        out_shape=out_shape,
        scratch_shapes=scratch,
        compiler_params=pltpu.CompilerParams(dimension_semantics=("arbitrary", "arbitrary"),
                                             vmem_limit_bytes=VMEM_LIMIT),
        name="prompt_layer",
    )(w["sinks2"][layer], h, p, tabs[0], tabs[1], tabs[2], bias, *_weight_args(w))


def _split3(x):
    hi = x.astype(BF16)
    r1 = x - hi.astype(F32)
    mid = r1.astype(BF16)
    return hi, mid, (r1 - mid.astype(F32)).astype(BF16)


def _sample_kernel(sinks_ref, h_ref, p_ref, st_ref, ck_ref, cv_ref, cos_ref, s1_ref, s2_ref, bmain_ref, bnew_ref, sel_ref,
                   ng_ref, gq_ref, gk_ref, ps_ref, win_ref, hn_ref, pm_ref,
                   wpp_ref, wpa_ref, wout_ref, wg_ref, wple_ref, *rest, nb, n_alias):
    y_ref, nkc_ref, nvc_ref, npool_ref, q_s, o_s, ya_s, nu_ref, nk_ref, nv_ref = rest[n_alias:]
    i = pl.program_id(0)
    nsteps = pl.num_programs(0)
    dec = h_ref.shape[0] // nb

    @pl.when(i == 0)
    def _():
        h = h_ref[...]
        xn = _rms_in(h, ng_ref[...])

        def proj(c0, c1):
            return _dot(xn, win_ref[:, c0:c1])

        nu_ref[...] = proj(C_U, C_ZP)
        zp = proj(C_ZP, C_Q)
        ya_s[...] = zp * _sigmoid(zp)

        cos, s1, s2t = cos_ref[...], s1_ref[...], s2_ref[...]
        hn = hn_ref[...]
        q_s[...] = _rope(_head_norm(proj(C_Q, C_K), hn, gq_ref[...]), cos, s1, s2t)
        nk_ref[...] = _rope(_head_norm(proj(C_K, C_V), hn, gk_ref[...]), cos, s1, s2t)
        nv_ref[...] = proj(C_V, C_ZA)

    b0 = pl.multiple_of(i * SB, SB)
    rpp = dec * SB
    nrow = N_HEADS * rpp
    lo = lax.broadcasted_iota(jnp.int32, (rpp, LANES), 1) < HEAD_DIM
    zero = jnp.zeros((rpp, LANES), F32)

    def rows_tb(ref, c0, c1):
        return jnp.concatenate([ref[pl.ds(pl.multiple_of(tt * nb + b0, SB), SB), c0:c1] for tt in range(dec)], axis=0)

    u_tb = rows_tb(nu_ref, 0, D_MODEL)
    seq = [st_ref[j] for j in range(POOL_STATE)] + [u_tb[tt * SB:(tt + 1) * SB] for tt in range(dec)]
    r_rows = []
    for tt in range(dec):
        e = POOL_STATE + tt
        cols = []
        for g, w in enumerate(POOL_WINDOWS):
            cs = slice(g * POOL_GROUP, (g + 1) * POOL_GROUP)
            acc = seq[e][:, cs]
            for d in range(1, w):
                acc = acc + seq[e - d][:, cs]
            cols.append(acc * (1.0 / w))
        r_rows.append(jnp.concatenate(cols, axis=1) - seq[e])
    ya_tb = _pool_map(jnp.concatenate(r_rows, axis=0), pm_ref, ps_ref[...])
    for tt in range(dec):
        rs = pl.ds(pl.multiple_of(tt * nb + b0, SB), SB)
        ya_s[rs, :] = ya_s[rs, :] * ya_tb[tt * SB:(tt + 1) * SB]

    slot = lax.broadcasted_iota(jnp.int32, (KV_WIDTH, WINDOW), 1)
    knew_f = rows_tb(nk_ref, 0, KV_WIDTH)
    vnew_f = rows_tb(nv_ref, 0, KV_WIDTH)

    def new_cache(src_ref, new_f, dst_ref):
        pieces3 = _split3(new_f)
        for bl in range(SB):
            cols = None
            for pc in pieces3:
                part = lax.dot_general(pc, sel_ref[bl], (((0,), (0,)), ((), ())), preferred_element_type=F32)
                cols = part if cols is None else cols + part
            dst_ref[bl] = jnp.where(slot < WINDOW - dec, pltpu.roll(src_ref[bl], WINDOW - dec, 1), cols)

    pieces = []
    for kh in range(N_KV_HEADS):
        for gp in range(GQA_GROUP // 2):
            c = kh * 2 + gp
            x = rows_tb(q_s, c * LANES, (c + 1) * LANES)
            rx = pltpu.roll(x, HEAD_DIM, 1)
            for e in range(2):
                src = x if e == kh % 2 else rx
                half = jnp.where(lo, src, 0.0) if kh % 2 == 0 else jnp.where(lo, 0.0, src)
                pieces.append(jnp.concatenate([half, zero] if kh < 2 else [zero, half], axis=1))
    lhs = jnp.concatenate(pieces, axis=0).astype(BF16)
    kmain = jnp.concatenate([ck_ref[bl].astype(BF16) for bl in range(SB)], axis=1)
    vmain = jnp.concatenate([cv_ref[bl].astype(BF16) for bl in range(SB)], axis=1)
    knew = knew_f.astype(BF16)
    vnew = vnew_f.astype(BF16)
    s_main = (_dot(lhs, kmain).reshape(N_HEADS, rpp, SB * WINDOW) + bmain_ref[...][None]).reshape(nrow, SB * WINDOW)
    s_new = (_dot_t(lhs, knew).reshape(N_HEADS, rpp, rpp) + bnew_ref[...][None]).reshape(nrow, rpp)

    new_cache(ck_ref, knew_f, nkc_ref)
    new_cache(cv_ref, vnew_f, nvc_ref)

    sink = jnp.concatenate([jnp.full((rpp, 1), sinks_ref[hh], F32) for hh in range(N_HEADS)], axis=0)
    mx = jnp.maximum(jnp.maximum(jnp.max(s_main, axis=-1, keepdims=True), jnp.max(s_new, axis=-1, keepdims=True)), sink)
    p_main = jnp.exp2(s_main - mx).astype(BF16)
    p_new = jnp.exp2(s_new - mx).astype(BF16)
    den = (jnp.sum(p_main.astype(F32), axis=-1, keepdims=True) + jnp.sum(p_new.astype(F32), axis=-1, keepdims=True)
           + jnp.exp2(sink - mx))
    o = (_dot_t(p_main, vmain) + _dot(p_new, vnew)) / den
    for kh in range(N_KV_HEADS):
        for gp in range(GQA_GROUP // 2):
            halves = []
            for e in range(2):
                hh = kh * GQA_GROUP + 2 * gp + e
                x = o[hh * rpp:(hh + 1) * rpp, (kh // 2) * LANES:(kh // 2 + 1) * LANES]
                halves.append(x if e == kh % 2 else pltpu.roll(x, HEAD_DIM, 1))
            dest = jnp.where(lo, halves[0], halves[1])
            c = kh * 2 + gp
            for tt in range(dec):
                o_s[pl.ds(pl.multiple_of(tt * nb + b0, SB), SB), c * LANES:(c + 1) * LANES] = dest[tt * SB:(tt + 1) * SB]

    for j in range(POOL_STATE):
        npool_ref[j] = seq[j + dec]

    @pl.when(i == nsteps - 1)
    def _():
        h = h_ref[...]
        xn = _rms_in(h, ng_ref[...])
        za = _dot(xn, win_ref[:, C_ZA:C_GA])
        yb = o_s[...] * (za * _sigmoid(za))
        y_ref[...] = _tail(h, p_ref[...], ya_s[...], yb, _dot(xn, win_ref[:, C_GA:C_GB]), _dot(xn, win_ref[:, C_GB:C_END]),
                           wpp_ref[...], wpa_ref[...], wout_ref[...], wg_ref[...], wple_ref[...])


def _sample_layer(h, p, st, ck, cv, tabs, bmain, bnew, sel, w, layer, stacked):
    R = h.shape[0]
    nb = ck.shape[1]
    full = lambda shape: pl.BlockSpec(shape, lambda i: (0,) * len(shape), pipeline_mode=pl.Buffered(1))
    cache = pl.BlockSpec((None, SB, KV_WIDTH, WINDOW), lambda i: (layer, i, 0, 0))
    state = pl.BlockSpec((None, POOL_STATE, SB, D_MODEL), lambda i: (layer, 0, i, 0))
    in_specs = [
        pl.BlockSpec(memory_space=pltpu.SMEM),
        full((R, D_MODEL)), full((R, PLE_DIM)), state, cache, cache,
        full((R, LANES)), full((R, LANES)), full((R, LANES)), full(bmain.shape), full(bnew.shape), full(sel.shape),
    ] + _weight_specs(layer)
    out_specs = [full((R, D_MODEL)), cache, cache, state]
    out_shape = [jax.ShapeDtypeStruct((R, D_MODEL), F32), jax.ShapeDtypeStruct(ck.shape, F32),
                 jax.ShapeDtypeStruct(cv.shape, F32), jax.ShapeDtypeStruct(st.shape, F32)]
    scratch = [pltpu.VMEM((R, D_MODEL), F32) for _ in range(4)] + [pltpu.VMEM((R, KV_WIDTH), F32) for _ in range(2)]
    args = [w["sinks2"][layer], h, p, st, ck, cv, tabs[0], tabs[1], tabs[2], bmain, bnew, sel] + _weight_args(w)
    aliases = {}
    if stacked is not None:
        aliases = {len(args) + k: 1 + k for k in range(len(stacked))}
        args += list(stacked)
        in_specs += [pl.BlockSpec(memory_space=pl.ANY)] * len(stacked)
    return pl.pallas_call(
        functools.partial(_sample_kernel, nb=nb, n_alias=len(aliases)),
        grid=(nb // SB,),
        in_specs=in_specs,
        out_specs=out_specs,
        out_shape=out_shape,
        scratch_shapes=scratch,
        input_output_aliases=aliases,
        compiler_params=pltpu.CompilerParams(dimension_semantics=("arbitrary",), vmem_limit_bytes=VMEM_LIMIT),
        name="sample_layer",
    )(*args)


def _rope_tables(pos):
    pos = np.asarray(pos, np.float64)
    inv_freq = ROPE_THETA ** (-np.arange(0, ROPE_DIM, 2, dtype=np.float64) / ROPE_DIM)
    ang = pos[:, None] * inv_freq[None, :]
    cos, sin = np.cos(ang), np.sin(ang)
    n = pos.shape[0]
    half = ROPE_DIM // 2
    c64 = np.concatenate([cos, cos, np.ones((n, HEAD_DIM - ROPE_DIM))], axis=1)
    a64 = np.concatenate([-sin, np.zeros((n, HEAD_DIM - half))], axis=1)
    b64 = np.concatenate([np.zeros((n, half)), sin, np.zeros((n, HEAD_DIM - ROPE_DIM))], axis=1)
    rep = LANES // HEAD_DIM
    return tuple(jnp.asarray(np.tile(x, (1, rep)), F32) for x in (c64, a64, b64))


def _prompt_bias():
    q = np.arange(WINDOW)[:, None]
    j = np.arange(WINDOW)[None, :]
    prev = np.where(j >= q, 0.0, NEG)
    cur = np.where(j <= q, 0.0, NEG)
    normal = np.concatenate([prev, cur], axis=1)
    first = np.concatenate([np.full_like(prev, NEG), cur], axis=1)
    return jnp.asarray(np.stack([normal, first]), F32)


def _sample_bias(dec):
    rq = np.arange(dec * SB)
    tq, bq = rq // SB, rq % SB
    cm = np.arange(SB * WINDOW)
    bm, jm = cm // WINDOW, cm % WINDOW
    main = np.where((bq[:, None] == bm[None, :]) & (jm[None, :] >= tq[:, None]), 0.0, NEG)
    new = np.where((bq[:, None] == bq[None, :]) & (tq[None, :] <= tq[:, None]), 0.0, NEG)
    sel = np.zeros((SB, dec * SB, WINDOW))
    for r in rq:
        sel[bq[r], r, WINDOW - dec + tq[r]] = 1.0
    return jnp.asarray(main, F32), jnp.asarray(new, F32), jnp.asarray(sel, BF16)


def _prep_weights(norm_g, w_in, q_norm_g, k_norm_g, sinks, pool_map, pool_scale,
                  w_proj_pool, w_proj_attn, w_out, w_ple, w_ple_gate):
    qscale = HEAD_DIM ** -0.5 * LOG2E
    blk = np.kron(np.eye(256 // HEAD_DIM), np.full((HEAD_DIM, HEAD_DIM), 1.0 / HEAD_DIM))
    return dict(
        sinks2=sinks * LOG2E,
        norm_g=norm_g[:, None, :],
        gq=jnp.tile(q_norm_g * qscale, (1, N_HEADS))[:, None, :],
        gk=jnp.tile(k_norm_g, (1, N_KV_HEADS))[:, None, :],
        pool_scale=pool_scale[:, None, :],
        w_in=w_in.astype(BF16),
        hn=jnp.asarray(blk, BF16),
        pool_map=pool_map.astype(BF16),
        w_pp=w_proj_pool.astype(BF16), w_pa=w_proj_attn.astype(BF16),
        w_out=w_out.astype(BF16), w_g=w_ple_gate.astype(BF16), w_ple=w_ple.astype(BF16),
    )


def kernel(x_prompt, x_sample, cache_k, cache_v, state_pool, p_prompt, p_sample, norm_g, w_in, q_norm_g, k_norm_g, sinks, pool_map, pool_scale, w_proj_pool, w_proj_attn, w_out, w_ple, w_ple_gate):
    depth = w_in.shape[0]
    B, T, _ = x_prompt.shape
    nb, dec, _ = x_sample.shape
    assert T % TQ == 0 and TQ % WINDOW == 0 and nb % SB == 0
    ptabs = _rope_tables(np.arange(T))
    stabs = _rope_tables(np.repeat(PAST_LEN + np.arange(dec), nb))
    pbias = _prompt_bias()
    bmain, bnew, sel = _sample_bias(dec)

    hp = x_prompt
    hs = x_sample.transpose(1, 0, 2).reshape(dec * nb, D_MODEL)
    ck_all = cache_k.transpose(0, 1, 3, 4, 2).reshape(depth, nb, KV_WIDTH, WINDOW)
    cv_all = cache_v.transpose(0, 1, 3, 4, 2).reshape(depth, nb, KV_WIDTH, WINDOW)
    st_all = state_pool.transpose(0, 2, 1, 3)
    stacked = None
    kp_l, vp_l, pp_l = [], [], []
    w = _prep_weights(norm_g, w_in, q_norm_g, k_norm_g, sinks, pool_map, pool_scale,
                      w_proj_pool, w_proj_attn, w_out, w_ple, w_ple_gate)
    for i in range(depth):
        hp, nk, nv, npool = _prompt_layer(hp, p_prompt, ptabs, pbias, w, i)
        kp_l.append(nk.reshape(B, WINDOW, N_KV_HEADS, HEAD_DIM))
        vp_l.append(nv.reshape(B, WINDOW, N_KV_HEADS, HEAD_DIM))
        pp_l.append(npool)

        ps_t = p_sample[i].transpose(1, 0, 2).reshape(dec * nb, PLE_DIM)
        hs, *stacked = _sample_layer(hs, ps_t, st_all, ck_all, cv_all, stabs, bmain, bnew, sel, w, i, stacked)
    uncache = lambda a: a.reshape(depth, nb, N_KV_HEADS, HEAD_DIM, WINDOW).transpose(0, 1, 4, 2, 3)
    new_k_sample, new_v_sample = uncache(stacked[0]), uncache(stacked[1])
    new_pool_sample = stacked[2].transpose(0, 2, 1, 3)
    y_sample = hs.reshape(dec, nb, D_MODEL).transpose(1, 0, 2)
    return (hp, y_sample, jnp.stack(kp_l), jnp.stack(vp_l), jnp.stack(pp_l),
            new_k_sample, new_v_sample, new_pool_sample)
```

```python
import functools
import math

import numpy as np
import jax
import jax.numpy as jnp
from jax import lax
from jax.experimental import pallas as pl
from jax.experimental.pallas import tpu as pltpu

D_MODEL = 1024
PLE_DIM = 256
POOL_WINDOWS = (2, 4, 8, 16)
POOL_GROUP = D_MODEL // len(POOL_WINDOWS)
POOL_STATE = max(POOL_WINDOWS) - 1
HEAD_DIM = 64
N_HEADS = D_MODEL // HEAD_DIM
N_KV_HEADS = 4
GQA_GROUP = N_HEADS // N_KV_HEADS
KV_WIDTH = N_KV_HEADS * HEAD_DIM
WINDOW = 128
ROPE_THETA = 500000.0
ROPE_DIM = HEAD_DIM // 4
EPS = 1e-6
PAST_LEN = 16384

C_U, C_ZP, C_Q, C_K, C_V, C_ZA, C_GA, C_GB, C_END = 0, 1024, 2048, 3072, 3328, 3584, 4608, 5632, 6656

LANES = 128
SUBLANES = 8
NEG = -1e30
LOG2E = math.log2(math.e)
VMEM_LIMIT = 60 * 1024 * 1024

TQ = 512
FCH = 256
SB = 8

F32 = jnp.float32
BF16 = jnp.bfloat16


def _sigmoid(x):
    return 1.0 / (1.0 + jnp.exp2(x * (-LOG2E)))


def _dot(a, b):
    return jnp.dot(a, b, preferred_element_type=F32)


def _dot_t(a, b):
    return lax.dot_general(a, b, (((1,), (1,)), ((), ())), preferred_element_type=F32)


def _rope(x, cos, s1, s2):
    outs = []
    for c in range(x.shape[1] // LANES):
        xc = x[:, c * LANES:(c + 1) * LANES]
        outs.append(xc * cos + pltpu.roll(xc, LANES - ROPE_DIM // 2, 1) * s1 + pltpu.roll(xc, ROPE_DIM // 2, 1) * s2)
    return outs[0] if len(outs) == 1 else jnp.concatenate(outs, axis=1)


def _head_norm(x, hn, gain):
    sq = (x * x).astype(BF16)
    ms = jnp.concatenate([_dot(sq[:, c * 256:(c + 1) * 256], hn) for c in range(x.shape[1] // 256)], axis=1)
    return x * lax.rsqrt(ms + EPS) * gain


def _rms_in(h, g):
    ms = jnp.mean(h * h, axis=-1, keepdims=True)
    return (h * lax.rsqrt(ms + EPS) * g).astype(BF16)


def _tail(h, p, ya, yb, ga, gb, wpp, wpa, wout, wg, wple):
    m = _sigmoid(ga) * _dot(ya.astype(BF16), wpp) + _sigmoid(gb) * _dot(yb.astype(BF16), wpa)
    h1 = h + _dot(m.astype(BF16), wout)
    gate = _sigmoid(_dot(h1.astype(BF16), wg))
    return h1 + gate * _dot(p.astype(BF16), wple)


def _pool_map(r, pm_ref, ps):
    rb = r.astype(BF16)
    mapped = jnp.concatenate([_dot(rb[:, g * POOL_GROUP:(g + 1) * POOL_GROUP], pm_ref[g])
                              for g in range(len(POOL_WINDOWS))], axis=1)
    return mapped * ps


def _prompt_kernel(sinks_ref, h_ref, p_ref, cos_ref, s1_ref, s2_ref, bias_ref,
                   ng_ref, gq_ref, gk_ref, ps_ref, win_ref, hn_ref, pm_ref,
                   wpp_ref, wpa_ref, wout_ref, wg_ref, wple_ref,
                   y_ref, nk_ref, nv_ref, npool_ref,
                   kall, vall, u_s, s2_s, s4_s, s8_s, o_s):
    t = pl.program_id(1)
    nqb = TQ // WINDOW
    lane = lax.broadcasted_iota(jnp.int32, (WINDOW, LANES), 1)
    lo = lane < HEAD_DIM

    @pl.when(t == 0)
    def _():
        kall[...] = jnp.zeros_like(kall)
        vall[:, :, 0:LANES] = jnp.zeros((N_KV_HEADS, (nqb + 1) * 256, LANES), BF16)
        ones_pat = jnp.concatenate([jnp.where(lo, 1.0, 0.0), jnp.where(lo, 0.0, 1.0)], axis=0).astype(BF16)
        for kh in range(N_KV_HEADS):
            for x in range(nqb + 1):
                vall[kh, x * 256:(x + 1) * 256, LANES:2 * LANES] = ones_pat
        u_s[0:16, :] = jnp.zeros((16, D_MODEL), F32)
        s2_s[0:16, :] = jnp.zeros((16, D_MODEL), F32)
        s4_s[0:16, :] = jnp.zeros((16, 768), F32)
        s8_s[0:16, :] = jnp.zeros((16, 512), F32)

    @pl.when(t > 0)
    def _():
        for kh in range(N_KV_HEADS):
            kall[kh, 0:256, :] = kall[kh, nqb * 256:(nqb + 1) * 256, :]
            vall[kh, 0:256, 0:LANES] = vall[kh, nqb * 256:(nqb + 1) * 256, 0:LANES]

    h = h_ref[0]
    xn = _rms_in(h, ng_ref[...])

    def proj(c0, c1):
        return _dot(xn, win_ref[:, c0:c1])

    res = {}

    def fillers():
        parts = []
        for c in range(D_MODEL // FCH):
            parts.append(proj(C_U + c * FCH, C_U + (c + 1) * FCH))
            yield
        u = jnp.concatenate(parts, axis=1)
        u_s[16:16 + TQ, :] = u
        s2 = u + u_s[15:15 + TQ, :]
        s2_s[16:16 + TQ, :] = s2
        s4 = s2[:, 256:] + s2_s[14:14 + TQ, 256:]
        s4_s[16:16 + TQ, :] = s4
        s8 = s4[:, 256:] + s4_s[12:12 + TQ, 256:]
        s8_s[16:16 + TQ, :] = s8
        s16 = s8[:, 256:] + s8_s[8:8 + TQ, 256:]
        pos1 = (t * TQ + 1 + lax.broadcasted_iota(jnp.int32, (TQ, 1), 0)).astype(F32)
        wins = (s2[:, :256], s4[:, :256], s8[:, :256], s16)
        r = jnp.concatenate([wins[g] * (1.0 / jnp.minimum(pos1, float(w))) for g, w in enumerate(POOL_WINDOWS)],
                            axis=1) - u
        npool_ref[0] = u_s[TQ + 1:TQ + 16, :]
        u_s[0:16, :] = u_s[TQ:TQ + 16, :]
        s2_s[0:16, :] = s2_s[TQ:TQ + 16, :]
        s4_s[0:16, :] = s4_s[TQ:TQ + 16, :]
        s8_s[0:16, :] = s8_s[TQ:TQ + 16, :]
        mapped = _pool_map(r, pm_ref, ps_ref[...])
        yield
        for name, c_lo in (("zp", C_ZP), ("za", C_ZA), ("ga", C_GA), ("gb", C_GB)):
            parts = []
            for c in range(D_MODEL // FCH):
                parts.append(proj(c_lo + c * FCH, c_lo + (c + 1) * FCH))
                yield
            res[name] = jnp.concatenate(parts, axis=1)
        yab = (mapped * (res["zp"] * _sigmoid(res["zp"]))).astype(BF16)
        pb = p_ref[0].astype(BF16)
        for name, lhs, w_ref in (("pp", yab, wpp_ref), ("ple", pb, wple_ref)):
            parts = []
            for c in range(D_MODEL // FCH):
                parts.append(_dot(lhs, w_ref[:, c * FCH:(c + 1) * FCH]))
                yield
            res[name] = jnp.concatenate(parts, axis=1)

    def tiled_bias(b):
        b4 = jnp.concatenate([b[:, :LANES], b[:, :LANES], b[:, LANES:], b[:, LANES:]], axis=1)
        return jnp.concatenate([b4, b4], axis=0)

    biases = [tiled_bias(jnp.where(t == 0, bias_ref[1], bias_ref[0]))] + [tiled_bias(bias_ref[0])] * (nqb - 1)
    chains = [(kh, n) for kh in range(N_KV_HEADS) for n in range(nqb)]
    n_fill = 7 * (D_MODEL // FCH) + 1
    fill = fillers()

    def scores(kh, n):
        rows = slice(n * WINDOW, (n + 1) * WINDOW)
        qs = jnp.concatenate([qb[rows, kh * 256:kh * 256 + LANES],
                              qb[rows, kh * 256 + LANES:(kh + 1) * 256]], axis=0)
        return _dot_t(qs, kall[kh, n * 256:n * 256 + 512, :]) + biases[n]

    q_raw, k_raw, v = proj(C_Q, C_K), proj(C_K, C_V), proj(C_V, C_ZA)
    pre = D_MODEL // FCH + 1
    for _ in range(pre):
        next(fill)
    done = pre
    cos, s1, s2t = cos_ref[...], s1_ref[...], s2_ref[...]
    hn = hn_ref[...]
    kr = _rope(_head_norm(k_raw, hn, gk_ref[...]), cos, s1, s2t)
    qb = _rope(_head_norm(q_raw, hn, gq_ref[...]), cos, s1, s2t).astype(BF16)
    nk_ref[0] = kr[TQ - WINDOW:, :]
    nv_ref[0] = v[TQ - WINDOW:, :]
    for x in range(nqb):
        rows = slice(x * WINDOW, (x + 1) * WINDOW)
        base = (x + 1) * 256
        for pr in range(N_KV_HEADS // 2):
            for src, dst, col in ((kr, kall, None), (v, vall, slice(0, LANES))):
                a = src[rows, pr * LANES:(pr + 1) * LANES]
                ra = pltpu.roll(a, HEAD_DIM, 1)
                parts = ((jnp.where(lo, a, 0.0), jnp.where(lo, 0.0, ra)),
                         (jnp.where(lo, ra, 0.0), jnp.where(lo, 0.0, a)))
                for e in range(2):
                    kh = 2 * pr + e
                    lh = jnp.concatenate(parts[e], axis=0).astype(BF16)
                    if col is None:
                        dst[kh, base:base + 256, :] = lh
                    else:
                        dst[kh, base:base + 256, col] = lh

    s_next = scores(*chains[0])
    for ci, (kh, n) in enumerate(chains):
        s = s_next
        if ci + 1 < len(chains):
            s_next = scores(*chains[ci + 1])
        want = pre + ((n_fill - pre) * (ci + 1)) // len(chains)
        while done < want:
            next(fill)
            done += 1
        rows = slice(n * WINDOW, (n + 1) * WINDOW)
        p_rows, sink_rows = [], []
        for rr in range(2):
            rs = slice(rr * WINDOW, (rr + 1) * WINDOW)
            pcols = [None] * 4
            sterm = []
            for e in range(2):
                sink = sinks_ref[kh * GQA_GROUP + 2 * rr + e]
                sp = s[rs, e * LANES:(e + 1) * LANES]
                sc = s[rs, 256 + e * LANES:256 + (e + 1) * LANES]
                mx = jnp.maximum(jnp.max(jnp.maximum(sp, sc), axis=-1, keepdims=True), sink)
                pcols[e] = jnp.exp2(sp - mx)
                pcols[2 + e] = jnp.exp2(sc - mx)
                sterm.append(jnp.exp2(sink - mx))
            p_rows.append(jnp.concatenate(pcols, axis=1))
            sink_rows.append(jnp.where(lo, sterm[0], sterm[1]))
        pmat = jnp.concatenate(p_rows, axis=0).astype(BF16)
        o2 = _dot(pmat, vall[kh, n * 256:n * 256 + 512, :])
        o = o2[:, :LANES] / (o2[:, LANES:] + jnp.concatenate(sink_rows, axis=0))
        o_s[rows, kh * 256:kh * 256 + LANES] = o[:WINDOW]
        o_s[rows, kh * 256 + LANES:(kh + 1) * 256] = o[WINDOW:]
    for _ in fill:
        pass

    za, ga, gb = res["za"], res["ga"], res["gb"]
    yb = o_s[...] * (za * _sigmoid(za))
    m = _sigmoid(ga) * res["pp"] + _sigmoid(gb) * _dot(yb.astype(BF16), wpa_ref[...])
    h1 = h + _dot(m.astype(BF16), wout_ref[...])
    gate = _sigmoid(_dot(h1.astype(BF16), wg_ref[...]))
    y_ref[0] = h1 + gate * res["ple"]


def _const_spec(shape):
    nd = len(shape)
    return pl.BlockSpec(shape, lambda *_: (0,) * nd, pipeline_mode=pl.Buffered(1))


def _layer_spec(shape, layer):
    nd = len(shape)
    return pl.BlockSpec((None,) + tuple(shape), lambda *_: (layer,) + (0,) * nd, pipeline_mode=pl.Buffered(1))


def _weight_specs(layer):
    ls = lambda *shape: _layer_spec(shape, layer)
    return [ls(1, D_MODEL), ls(1, D_MODEL), ls(1, KV_WIDTH), ls(1, D_MODEL),
            ls(D_MODEL, C_END), _const_spec((256, 256)), ls(4, POOL_GROUP, POOL_GROUP),
            ls(D_MODEL, D_MODEL), ls(D_MODEL, D_MODEL), ls(D_MODEL, D_MODEL), ls(D_MODEL, D_MODEL), ls(PLE_DIM, D_MODEL)]


def _weight_args(w):
    return [w["norm_g"], w["gq"], w["gk"], w["pool_scale"], w["w_in"], w["hn"], w["pool_map"],
            w["w_pp"], w["w_pa"], w["w_out"], w["w_g"], w["w_ple"]]


def _prompt_layer(h, p, tabs, bias, w, layer):
    B, T, _ = h.shape
    nt = T // TQ
    nqb = TQ // WINDOW
    row = lambda width: pl.BlockSpec((1, TQ, width), lambda b, t: (b, t, 0))
    tab = pl.BlockSpec((TQ, LANES), lambda b, t: (t, 0))
    in_specs = [
        pl.BlockSpec(memory_space=pltpu.SMEM),
        row(D_MODEL), pl.BlockSpec((None, 1, TQ, PLE_DIM), lambda b, t: (layer, b, t, 0)), tab, tab, tab,
        _const_spec(bias.shape),
    ] + _weight_specs(layer)
    out_specs = [
        row(D_MODEL),
        pl.BlockSpec((1, WINDOW, KV_WIDTH), lambda b, t: (b, 0, 0)),
        pl.BlockSpec((1, WINDOW, KV_WIDTH), lambda b, t: (b, 0, 0)),
        pl.BlockSpec((1, POOL_STATE, D_MODEL), lambda b, t: (b, 0, 0)),
    ]
    out_shape = [
        jax.ShapeDtypeStruct((B, T, D_MODEL), F32),
        jax.ShapeDtypeStruct((B, WINDOW, KV_WIDTH), F32),
        jax.ShapeDtypeStruct((B, WINDOW, KV_WIDTH), F32),
        jax.ShapeDtypeStruct((B, POOL_STATE, D_MODEL), F32),
    ]
    scratch = [
        pltpu.VMEM((N_KV_HEADS, (nqb + 1) * 256, LANES), BF16),
        pltpu.VMEM((N_KV_HEADS, (nqb + 1) * 256, 2 * LANES), BF16),
        pltpu.VMEM((TQ + 16, D_MODEL), F32),
        pltpu.VMEM((TQ + 16, D_MODEL), F32),
        pltpu.VMEM((TQ + 16, 768), F32),
        pltpu.VMEM((TQ + 16, 512), F32),
        pltpu.VMEM((TQ, D_MODEL), F32),
    ]
    return pl.pallas_call(
        _prompt_kernel,
        grid=(B, nt),
        in_specs=in_specs,
        out_specs=out_specs,
        out_shape=out_shape,
        scratch_shapes=scratch,
        compiler_params=pltpu.CompilerParams(dimension_semantics=("arbitrary", "arbitrary"),
                                             vmem_limit_bytes=VMEM_LIMIT),
        name="prompt_layer",
    )(w["sinks2"][layer], h, p, tabs[0], tabs[1], tabs[2], bias, *_weight_args(w))


def _split3(x):
    hi = x.astype(BF16)
    r1 = x - hi.astype(F32)
    mid = r1.astype(BF16)
    return hi, mid, (r1 - mid.astype(F32)).astype(BF16)


def _sample_kernel(sinks_ref, h_ref, p_ref, st_ref, ck_ref, cv_ref, cos_ref, s1_ref, s2_ref, bmain_ref, bnew_ref, sel_ref,
                   ng_ref, gq_ref, gk_ref, ps_ref, win_ref, hn_ref, pm_ref,
                   wpp_ref, wpa_ref, wout_ref, wg_ref, wple_ref, *rest, nb, n_alias):
    y_ref, nkc_ref, nvc_ref, npool_ref, q_s, o_s, ya_s, nu_ref, nk_ref, nv_ref = rest[n_alias:]
    i = pl.program_id(0)
    nsteps = pl.num_programs(0)
    dec = h_ref.shape[0] // nb

    @pl.when(i == 0)
    def _():
        h = h_ref[...]
        xn = _rms_in(h, ng_ref[...])

        def proj(c0, c1):
            return _dot(xn, win_ref[:, c0:c1])

        nu_ref[...] = proj(C_U, C_ZP)
        zp = proj(C_ZP, C_Q)
        ya_s[...] = zp * _sigmoid(zp)

        cos, s1, s2t = cos_ref[...], s1_ref[...], s2_ref[...]
        hn = hn_ref[...]
        q_s[...] = _rope(_head_norm(proj(C_Q, C_K), hn, gq_ref[...]), cos, s1, s2t)
        nk_ref[...] = _rope(_head_norm(proj(C_K, C_V), hn, gk_ref[...]), cos, s1, s2t)
        nv_ref[...] = proj(C_V, C_ZA)

    b0 = pl.multiple_of(i * SB, SB)
    rpp = dec * SB
    nrow = N_HEADS * rpp
    lo = lax.broadcasted_iota(jnp.int32, (rpp, LANES), 1) < HEAD_DIM
    zero = jnp.zeros((rpp, LANES), F32)

    def rows_tb(ref, c0, c1):
        return jnp.concatenate([ref[pl.ds(pl.multiple_of(tt * nb + b0, SB), SB), c0:c1] for tt in range(dec)], axis=0)

    u_tb = rows_tb(nu_ref, 0, D_MODEL)
    seq = [st_ref[j] for j in range(POOL_STATE)] + [u_tb[tt * SB:(tt + 1) * SB] for tt in range(dec)]
    r_rows = []
    for tt in range(dec):
        e = POOL_STATE + tt
        cols = []
        for g, w in enumerate(POOL_WINDOWS):
            cs = slice(g * POOL_GROUP, (g + 1) * POOL_GROUP)
            acc = seq[e][:, cs]
            for d in range(1, w):
                acc = acc + seq[e - d][:, cs]
            cols.append(acc * (1.0 / w))
        r_rows.append(jnp.concatenate(cols, axis=1) - seq[e])
    ya_tb = _pool_map(jnp.concatenate(r_rows, axis=0), pm_ref, ps_ref[...])
    for tt in range(dec):
        rs = pl.ds(pl.multiple_of(tt * nb + b0, SB), SB)
        ya_s[rs, :] = ya_s[rs, :] * ya_tb[tt * SB:(tt + 1) * SB]

    slot = lax.broadcasted_iota(jnp.int32, (KV_WIDTH, WINDOW), 1)
    knew_f = rows_tb(nk_ref, 0, KV_WIDTH)
    vnew_f = rows_tb(nv_ref, 0, KV_WIDTH)

    def new_cache(src_ref, new_f, dst_ref):
        pieces3 = _split3(new_f)
        for bl in range(SB):
            cols = None
            for pc in pieces3:
                part = lax.dot_general(pc, sel_ref[bl], (((0,), (0,)), ((), ())), preferred_element_type=F32)
                cols = part if cols is None else cols + part
            dst_ref[bl] = jnp.where(slot < WINDOW - dec, pltpu.roll(src_ref[bl], WINDOW - dec, 1), cols)

    pieces = []
    for kh in range(N_KV_HEADS):
        for gp in range(GQA_GROUP // 2):
            c = kh * 2 + gp
            x = rows_tb(q_s, c * LANES, (c + 1) * LANES)
            rx = pltpu.roll(x, HEAD_DIM, 1)
            for e in range(2):
                src = x if e == kh % 2 else rx
                half = jnp.where(lo, src, 0.0) if kh % 2 == 0 else jnp.where(lo, 0.0, src)
                pieces.append(jnp.concatenate([half, zero] if kh < 2 else [zero, half], axis=1))
    lhs = jnp.concatenate(pieces, axis=0).astype(BF16)
    kmain = jnp.concatenate([ck_ref[bl].astype(BF16) for bl in range(SB)], axis=1)
    vmain = jnp.concatenate([cv_ref[bl].astype(BF16) for bl in range(SB)], axis=1)
    knew = knew_f.astype(BF16)
    vnew = vnew_f.astype(BF16)
    s_main = (_dot(lhs, kmain).reshape(N_HEADS, rpp, SB * WINDOW) + bmain_ref[...][None]).reshape(nrow, SB * WINDOW)
    s_new = (_dot_t(lhs, knew).reshape(N_HEADS, rpp, rpp) + bnew_ref[...][None]).reshape(nrow, rpp)

    new_cache(ck_ref, knew_f, nkc_ref)
    new_cache(cv_ref, vnew_f, nvc_ref)

    sink = jnp.concatenate([jnp.full((rpp, 1), sinks_ref[hh], F32) for hh in range(N_HEADS)], axis=0)
    mx = jnp.maximum(jnp.maximum(jnp.max(s_main, axis=-1, keepdims=True), jnp.max(s_new, axis=-1, keepdims=True)), sink)
    p_main = jnp.exp2(s_main - mx).astype(BF16)
    p_new = jnp.exp2(s_new - mx).astype(BF16)
    den = (jnp.sum(p_main.astype(F32), axis=-1, keepdims=True) + jnp.sum(p_new.astype(F32), axis=-1, keepdims=True)
           + jnp.exp2(sink - mx))
    o = (_dot_t(p_main, vmain) + _dot(p_new, vnew)) / den
    for kh in range(N_KV_HEADS):
        for gp in range(GQA_GROUP // 2):
            halves = []
            for e in range(2):
                hh = kh * GQA_GROUP + 2 * gp + e
                x = o[hh * rpp:(hh + 1) * rpp, (kh // 2) * LANES:(kh // 2 + 1) * LANES]
                halves.append(x if e == kh % 2 else pltpu.roll(x, HEAD_DIM, 1))
            dest = jnp.where(lo, halves[0], halves[1])
            c = kh * 2 + gp
            for tt in range(dec):
                o_s[pl.ds(pl.multiple_of(tt * nb + b0, SB), SB), c * LANES:(c + 1) * LANES] = dest[tt * SB:(tt + 1) * SB]

    for j in range(POOL_STATE):
        npool_ref[j] = seq[j + dec]

    @pl.when(i == nsteps - 1)
    def _():
        h = h_ref[...]
        xn = _rms_in(h, ng_ref[...])
        za = _dot(xn, win_ref[:, C_ZA:C_GA])
        yb = o_s[...] * (za * _sigmoid(za))
        y_ref[...] = _tail(h, p_ref[...], ya_s[...], yb, _dot(xn, win_ref[:, C_GA:C_GB]), _dot(xn, win_ref[:, C_GB:C_END]),
                           wpp_ref[...], wpa_ref[...], wout_ref[...], wg_ref[...], wple_ref[...])


def _sample_layer(h, p, st, ck, cv, tabs, bmain, bnew, sel, w, layer, stacked):
    R = h.shape[0]
    nb = ck.shape[1]
    full = lambda shape: pl.BlockSpec(shape, lambda i: (0,) * len(shape), pipeline_mode=pl.Buffered(1))
    cache = pl.BlockSpec((None, SB, KV_WIDTH, WINDOW), lambda i: (layer, i, 0, 0))
    state = pl.BlockSpec((None, POOL_STATE, SB, D_MODEL), lambda i: (layer, 0, i, 0))
    in_specs = [
        pl.BlockSpec(memory_space=pltpu.SMEM),
        full((R, D_MODEL)), full((R, PLE_DIM)), state, cache, cache,
        full((R, LANES)), full((R, LANES)), full((R, LANES)), full(bmain.shape), full(bnew.shape), full(sel.shape),
    ] + _weight_specs(layer)
    out_specs = [full((R, D_MODEL)), cache, cache, state]
    out_shape = [jax.ShapeDtypeStruct((R, D_MODEL), F32), jax.ShapeDtypeStruct(ck.shape, F32),
                 jax.ShapeDtypeStruct(cv.shape, F32), jax.ShapeDtypeStruct(st.shape, F32)]
    scratch = [pltpu.VMEM((R, D_MODEL), F32) for _ in range(4)] + [pltpu.VMEM((R, KV_WIDTH), F32) for _ in range(2)]
    args = [w["sinks2"][layer], h, p, st, ck, cv, tabs[0], tabs[1], tabs[2], bmain, bnew, sel] + _weight_args(w)
    aliases = {}
    if stacked is not None:
        aliases = {len(args) + k: 1 + k for k in range(len(stacked))}
        args += list(stacked)
        in_specs += [pl.BlockSpec(memory_space=pl.ANY)] * len(stacked)
    return pl.pallas_call(
        functools.partial(_sample_kernel, nb=nb, n_alias=len(aliases)),
        grid=(nb // SB,),
        in_specs=in_specs,
        out_specs=out_specs,
        out_shape=out_shape,
        scratch_shapes=scratch,
        input_output_aliases=aliases,
        compiler_params=pltpu.CompilerParams(dimension_semantics=("arbitrary",), vmem_limit_bytes=VMEM_LIMIT),
        name="sample_layer",
    )(*args)


def _rope_tables(pos):
    pos = np.asarray(pos, np.float64)
    inv_freq = ROPE_THETA ** (-np.arange(0, ROPE_DIM, 2, dtype=np.float64) / ROPE_DIM)
    ang = pos[:, None] * inv_freq[None, :]
    cos, sin = np.cos(ang), np.sin(ang)
    n = pos.shape[0]
    half = ROPE_DIM // 2
    c64 = np.concatenate([cos, cos, np.ones((n, HEAD_DIM - ROPE_DIM))], axis=1)
    a64 = np.concatenate([-sin, np.zeros((n, HEAD_DIM - half))], axis=1)
    b64 = np.concatenate([np.zeros((n, half)), sin, np.zeros((n, HEAD_DIM - ROPE_DIM))], axis=1)
    rep = LANES // HEAD_DIM
    return tuple(jnp.asarray(np.tile(x, (1, rep)), F32) for x in (c64, a64, b64))


def _prompt_bias():
    q = np.arange(WINDOW)[:, None]
    j = np.arange(WINDOW)[None, :]
    prev = np.where(j >= q, 0.0, NEG)
    cur = np.where(j <= q, 0.0, NEG)
    normal = np.concatenate([prev, cur], axis=1)
    first = np.concatenate([np.full_like(prev, NEG), cur], axis=1)
    return jnp.asarray(np.stack([normal, first]), F32)


def _sample_bias(dec):
    rq = np.arange(dec * SB)
    tq, bq = rq // SB, rq % SB
    cm = np.arange(SB * WINDOW)
    bm, jm = cm // WINDOW, cm % WINDOW
    main = np.where((bq[:, None] == bm[None, :]) & (jm[None, :] >= tq[:, None]), 0.0, NEG)
    new = np.where((bq[:, None] == bq[None, :]) & (tq[None, :] <= tq[:, None]), 0.0, NEG)
    sel = np.zeros((SB, dec * SB, WINDOW))
    for r in rq:
        sel[bq[r], r, WINDOW - dec + tq[r]] = 1.0
    return jnp.asarray(main, F32), jnp.asarray(new, F32), jnp.asarray(sel, BF16)


def _prep_weights(norm_g, w_in, q_norm_g, k_norm_g, sinks, pool_map, pool_scale,
                  w_proj_pool, w_proj_attn, w_out, w_ple, w_ple_gate):
    qscale = HEAD_DIM ** -0.5 * LOG2E
    blk = np.kron(np.eye(256 // HEAD_DIM), np.full((HEAD_DIM, HEAD_DIM), 1.0 / HEAD_DIM))
    return dict(
        sinks2=sinks * LOG2E,
        norm_g=norm_g[:, None, :],
        gq=jnp.tile(q_norm_g * qscale, (1, N_HEADS))[:, None, :],
        gk=jnp.tile(k_norm_g, (1, N_KV_HEADS))[:, None, :],
        pool_scale=pool_scale[:, None, :],
        w_in=w_in.astype(BF16),
        hn=jnp.asarray(blk, BF16),
        pool_map=pool_map.astype(BF16),
        w_pp=w_proj_pool.astype(BF16), w_pa=w_proj_attn.astype(BF16),
        w_out=w_out.astype(BF16), w_g=w_ple_gate.astype(BF16), w_ple=w_ple.astype(BF16),
    )


def kernel(x_prompt, x_sample, cache_k, cache_v, state_pool, p_prompt, p_sample, norm_g, w_in, q_norm_g, k_norm_g, sinks, pool_map, pool_scale, w_proj_pool, w_proj_attn, w_out, w_ple, w_ple_gate):
    depth = w_in.shape[0]
    B, T, _ = x_prompt.shape
    nb, dec, _ = x_sample.shape
    assert T % TQ == 0 and TQ % WINDOW == 0 and nb % SB == 0
    ptabs = _rope_tables(np.arange(T))
    stabs = _rope_tables(np.repeat(PAST_LEN + np.arange(dec), nb))
    pbias = _prompt_bias()
    bmain, bnew, sel = _sample_bias(dec)

    hp = x_prompt
    hs = x_sample.transpose(1, 0, 2).reshape(dec * nb, D_MODEL)
    ck_all = cache_k.transpose(0, 1, 3, 4, 2).reshape(depth, nb, KV_WIDTH, WINDOW)
    cv_all = cache_v.transpose(0, 1, 3, 4, 2).reshape(depth, nb, KV_WIDTH, WINDOW)
    st_all = state_pool.transpose(0, 2, 1, 3)
    stacked = None
    kp_l, vp_l, pp_l = [], [], []
    w = _prep_weights(norm_g, w_in, q_norm_g, k_norm_g, sinks, pool_map, pool_scale,
                      w_proj_pool, w_proj_attn, w_out, w_ple, w_ple_gate)
    for i in range(depth):
        hp, nk, nv, npool = _prompt_layer(hp, p_prompt, ptabs, pbias, w, i)
        kp_l.append(nk.reshape(B, WINDOW, N_KV_HEADS, HEAD_DIM))
        vp_l.append(nv.reshape(B, WINDOW, N_KV_HEADS, HEAD_DIM))
        pp_l.append(npool)

        ps_t = p_sample[i].transpose(1, 0, 2).reshape(dec * nb, PLE_DIM)
        hs, *stacked = _sample_layer(hs, ps_t, st_all, ck_all, cv_all, stabs, bmain, bnew, sel, w, i, stacked)
    uncache = lambda a: a.reshape(depth, nb, N_KV_HEADS, HEAD_DIM, WINDOW).transpose(0, 1, 4, 2, 3)
    new_k_sample, new_v_sample = uncache(stacked[0]), uncache(stacked[1])
    new_pool_sample = stacked[2].transpose(0, 2, 1, 3)
    y_sample = hs.reshape(dec, nb, D_MODEL).transpose(1, 0, 2)
    return (hp, y_sample, jnp.stack(kp_l), jnp.stack(vp_l), jnp.stack(pp_l),
            new_k_sample, new_v_sample, new_pool_sample)
```

```python
import functools
import math

import numpy as np
import jax
import jax.numpy as jnp
from jax import lax
from jax.experimental import pallas as pl
from jax.experimental.pallas import tpu as pltpu

D_MODEL = 1024
PLE_DIM = 256
POOL_WINDOWS = (2, 4, 8, 16)
POOL_GROUP = D_MODEL // len(POOL_WINDOWS)
POOL_STATE = max(POOL_WINDOWS) - 1
HEAD_DIM = 64
N_HEADS = D_MODEL // HEAD_DIM
N_KV_HEADS = 4
GQA_GROUP = N_HEADS // N_KV_HEADS
KV_WIDTH = N_KV_HEADS * HEAD_DIM
WINDOW = 128
ROPE_THETA = 500000.0
ROPE_DIM = HEAD_DIM // 4
EPS = 1e-6
PAST_LEN = 16384

C_U, C_ZP, C_Q, C_K, C_V, C_ZA, C_GA, C_GB, C_END = 0, 1024, 2048, 3072, 3328, 3584, 4608, 5632, 6656

LANES = 128
SUBLANES = 8
NEG = -1e30
LOG2E = math.log2(math.e)
VMEM_LIMIT = 60 * 1024 * 1024

TQ = 512
FCH = 256
WCH = 64
SB = 8

F32 = jnp.float32
BF16 = jnp.bfloat16


def _sigmoid(x):
    return 1.0 / (1.0 + jnp.exp2(x * (-LOG2E)))


def _dot(a, b):
    return jnp.dot(a, b, preferred_element_type=F32)


def _dot_t(a, b):
    return lax.dot_general(a, b, (((1,), (1,)), ((), ())), preferred_element_type=F32)


def _rope(x, cos, s1, s2):
    outs = []
    for c in range(x.shape[1] // LANES):
        xc = x[:, c * LANES:(c + 1) * LANES]
        outs.append(xc * cos + pltpu.roll(xc, LANES - ROPE_DIM // 2, 1) * s1 + pltpu.roll(xc, ROPE_DIM // 2, 1) * s2)
    return outs[0] if len(outs) == 1 else jnp.concatenate(outs, axis=1)


def _head_norm(x, hn, gain):
    sq = (x * x).astype(BF16)
    ms = jnp.concatenate([_dot(sq[:, c * 256:(c + 1) * 256], hn) for c in range(x.shape[1] // 256)], axis=1)
    return x * lax.rsqrt(ms + EPS) * gain


def _rms_in(h, g):
    ms = jnp.mean(h * h, axis=-1, keepdims=True)
    return (h * lax.rsqrt(ms + EPS) * g).astype(BF16)


def _tail(h, p, ya, yb, ga, gb, wpp, wpa, wout, wg, wple):
    m = _sigmoid(ga) * _dot(ya.astype(BF16), wpp) + _sigmoid(gb) * _dot(yb.astype(BF16), wpa)
    h1 = h + _dot(m.astype(BF16), wout)
    gate = _sigmoid(_dot(h1.astype(BF16), wg))
    return h1 + gate * _dot(p.astype(BF16), wple)


def _pool_map(r, pm_ref, ps):
    rb = r.astype(BF16)
    mapped = jnp.concatenate([_dot(rb[:, g * POOL_GROUP:(g + 1) * POOL_GROUP], pm_ref[g])
                              for g in range(len(POOL_WINDOWS))], axis=1)
    return mapped * ps


def _prompt_kernel(sinks_ref, h_ref, p_ref, cos_ref, s1_ref, s2_ref, bias_ref,
                   ng_ref, gq_ref, gk_ref, ps_ref, winf_hbm, hn_ref, pm_ref,
                   wpp_ref, wpa_ref, wout_ref, wg_ref, wple_ref,
                   y_ref, nk_ref, nv_ref, npool_ref, winb_hbm,
                   kall, vall, u_s, s2_s, s4_s, s8_s, o_s, win_ref, stg, sem, *, layer):
    t = pl.program_id(1)
    first_step = jnp.logical_and(pl.program_id(0) == 0, t == 0)
    last_step = jnp.logical_and(pl.program_id(0) == pl.num_programs(0) - 1, t == pl.num_programs(1) - 1)
    nqb = TQ // WINDOW

    def stage(c):
        return pltpu.make_async_copy(winf_hbm.at[layer, pl.ds(c * WCH, WCH), :], stg.at[c % 2], sem.at[c % 2])

    publish = pltpu.make_async_copy(win_ref, winb_hbm, sem.at[2])

    @pl.when(first_step)
    def _():
        nch = D_MODEL // WCH
        stage(0).start()
        for c in range(nch):
            if c + 1 < nch:
                stage(c + 1).start()
            stage(c).wait()
            win_ref[c * WCH:(c + 1) * WCH, :] = stg[c % 2].astype(BF16)
        publish.start()
    lane = lax.broadcasted_iota(jnp.int32, (WINDOW, LANES), 1)
    lo = lane < HEAD_DIM

    @pl.when(t == 0)
    def _():
        kall[...] = jnp.zeros_like(kall)
        vall[:, :, 0:LANES] = jnp.zeros((N_KV_HEADS, (nqb + 1) * 256, LANES), BF16)
        ones_pat = jnp.concatenate([jnp.where(lo, 1.0, 0.0), jnp.where(lo, 0.0, 1.0)], axis=0).astype(BF16)
        for kh in range(N_KV_HEADS):
            for x in range(nqb + 1):
                vall[kh, x * 256:(x + 1) * 256, LANES:2 * LANES] = ones_pat
        u_s[0:16, :] = jnp.zeros((16, D_MODEL), F32)
        s2_s[0:16, :] = jnp.zeros((16, D_MODEL), F32)
        s4_s[0:16, :] = jnp.zeros((16, 768), F32)
        s8_s[0:16, :] = jnp.zeros((16, 512), F32)

    @pl.when(t > 0)
    def _():
        for kh in range(N_KV_HEADS):
            kall[kh, 0:256, :] = kall[kh, nqb * 256:(nqb + 1) * 256, :]
            vall[kh, 0:256, 0:LANES] = vall[kh, nqb * 256:(nqb + 1) * 256, 0:LANES]

    h = h_ref[0]
    xn = _rms_in(h, ng_ref[...])

    def proj(c0, c1):
        return _dot(xn, win_ref[:, c0:c1])

    res = {}

    def fillers():
        parts = []
        for c in range(D_MODEL // FCH):
            parts.append(proj(C_U + c * FCH, C_U + (c + 1) * FCH))
            yield
        u = jnp.concatenate(parts, axis=1)
        u_s[16:16 + TQ, :] = u
        s2 = u + u_s[15:15 + TQ, :]
        s2_s[16:16 + TQ, :] = s2
        s4 = s2[:, 256:] + s2_s[14:14 + TQ, 256:]
        s4_s[16:16 + TQ, :] = s4
        s8 = s4[:, 256:] + s4_s[12:12 + TQ, 256:]
        s8_s[16:16 + TQ, :] = s8
        s16 = s8[:, 256:] + s8_s[8:8 + TQ, 256:]
        pos1 = (t * TQ + 1 + lax.broadcasted_iota(jnp.int32, (TQ, 1), 0)).astype(F32)
        wins = (s2[:, :256], s4[:, :256], s8[:, :256], s16)
        r = jnp.concatenate([wins[g] * (1.0 / jnp.minimum(pos1, float(w))) for g, w in enumerate(POOL_WINDOWS)],
                            axis=1) - u
        npool_ref[0] = u_s[TQ + 1:TQ + 16, :]
        u_s[0:16, :] = u_s[TQ:TQ + 16, :]
        s2_s[0:16, :] = s2_s[TQ:TQ + 16, :]
        s4_s[0:16, :] = s4_s[TQ:TQ + 16, :]
        s8_s[0:16, :] = s8_s[TQ:TQ + 16, :]
        mapped = _pool_map(r, pm_ref, ps_ref[...])
        yield
        for name, c_lo in (("zp", C_ZP), ("za", C_ZA), ("ga", C_GA), ("gb", C_GB)):
            parts = []
            for c in range(D_MODEL // FCH):
                parts.append(proj(c_lo + c * FCH, c_lo + (c + 1) * FCH))
                yield
            res[name] = jnp.concatenate(parts, axis=1)
        yab = (mapped * (res["zp"] * _sigmoid(res["zp"]))).astype(BF16)
        pb = p_ref[0].astype(BF16)
        for name, lhs, w_ref in (("pp", yab, wpp_ref), ("ple", pb, wple_ref)):
            parts = []
            for c in range(D_MODEL // FCH):
                parts.append(_dot(lhs, w_ref[:, c * FCH:(c + 1) * FCH]))
                yield
            res[name] = jnp.concatenate(parts, axis=1)

    def tiled_bias(b):
        b4 = jnp.concatenate([b[:, :LANES], b[:, :LANES], b[:, LANES:], b[:, LANES:]], axis=1)
        return jnp.concatenate([b4, b4], axis=0)

    biases = [tiled_bias(jnp.where(t == 0, bias_ref[1], bias_ref[0]))] + [tiled_bias(bias_ref[0])] * (nqb - 1)
    chains = [(kh, n) for kh in range(N_KV_HEADS) for n in range(nqb)]
    n_fill = 7 * (D_MODEL // FCH) + 1
    fill = fillers()

    def scores(kh, n):
        rows = slice(n * WINDOW, (n + 1) * WINDOW)
        qs = jnp.concatenate([qb[rows, kh * 256:kh * 256 + LANES],
                              qb[rows, kh * 256 + LANES:(kh + 1) * 256]], axis=0)
        return _dot_t(qs, kall[kh, n * 256:n * 256 + 512, :]) + biases[n]

    q_raw, k_raw, v = proj(C_Q, C_K), proj(C_K, C_V), proj(C_V, C_ZA)
    pre = D_MODEL // FCH + 1
    for _ in range(pre):
        next(fill)
    done = pre
    cos, s1, s2t = cos_ref[...], s1_ref[...], s2_ref[...]
    hn = hn_ref[...]
    kr = _rope(_head_norm(k_raw, hn, gk_ref[...]), cos, s1, s2t)
    qb = _rope(_head_norm(q_raw, hn, gq_ref[...]), cos, s1, s2t).astype(BF16)
    nk_ref[0] = kr[TQ - WINDOW:, :]
    nv_ref[0] = v[TQ - WINDOW:, :]
    for x in range(nqb):
        rows = slice(x * WINDOW, (x + 1) * WINDOW)
        base = (x + 1) * 256
        for pr in range(N_KV_HEADS // 2):
            for src, dst, col in ((kr, kall, None), (v, vall, slice(0, LANES))):
                a = src[rows, pr * LANES:(pr + 1) * LANES]
                ra = pltpu.roll(a, HEAD_DIM, 1)
                parts = ((jnp.where(lo, a, 0.0), jnp.where(lo, 0.0, ra)),
                         (jnp.where(lo, ra, 0.0), jnp.where(lo, 0.0, a)))
                for e in range(2):
                    kh = 2 * pr + e
                    lh = jnp.concatenate(parts[e], axis=0).astype(BF16)
                    if col is None:
                        dst[kh, base:base + 256, :] = lh
                    else:
                        dst[kh, base:base + 256, col] = lh

    s_next = scores(*chains[0])
    for ci, (kh, n) in enumerate(chains):
        s = s_next
        if ci + 1 < len(chains):
            s_next = scores(*chains[ci + 1])
        want = pre + ((n_fill - pre) * (ci + 1)) // len(chains)
        while done < want:
            next(fill)
            done += 1
        rows = slice(n * WINDOW, (n + 1) * WINDOW)
        p_rows, sink_rows = [], []
        for rr in range(2):
            rs = slice(rr * WINDOW, (rr + 1) * WINDOW)
            pcols = [None] * 4
            sterm = []
            for e in range(2):
                sink = sinks_ref[kh * GQA_GROUP + 2 * rr + e]
                sp = s[rs, e * LANES:(e + 1) * LANES]
                sc = s[rs, 256 + e * LANES:256 + (e + 1) * LANES]
                mx = jnp.maximum(jnp.max(jnp.maximum(sp, sc), axis=-1, keepdims=True), sink)
                pcols[e] = jnp.exp2(sp - mx)
                pcols[2 + e] = jnp.exp2(sc - mx)
                sterm.append(jnp.exp2(sink - mx))
            p_rows.append(jnp.concatenate(pcols, axis=1))
            sink_rows.append(jnp.where(lo, sterm[0], sterm[1]))
        pmat = jnp.concatenate(p_rows, axis=0).astype(BF16)
        o2 = _dot(pmat, vall[kh, n * 256:n * 256 + 512, :])
        o = o2[:, :LANES] / (o2[:, LANES:] + jnp.concatenate(sink_rows, axis=0))
        o_s[rows, kh * 256:kh * 256 + LANES] = o[:WINDOW]
        o_s[rows, kh * 256 + LANES:(kh + 1) * 256] = o[WINDOW:]
    for _ in fill:
        pass

    za, ga, gb = res["za"], res["ga"], res["gb"]
    yb = o_s[...] * (za * _sigmoid(za))
    m = _sigmoid(ga) * res["pp"] + _sigmoid(gb) * _dot(yb.astype(BF16), wpa_ref[...])
    h1 = h + _dot(m.astype(BF16), wout_ref[...])
    gate = _sigmoid(_dot(h1.astype(BF16), wg_ref[...]))
    y_ref[0] = h1 + gate * res["ple"]

    @pl.when(last_step)
    def _():
        publish.wait()


def _const_spec(shape):
    nd = len(shape)
    return pl.BlockSpec(shape, lambda *_: (0,) * nd, pipeline_mode=pl.Buffered(1))


def _layer_spec(shape, layer):
    nd = len(shape)
    return pl.BlockSpec((None,) + tuple(shape), lambda *_: (layer,) + (0,) * nd, pipeline_mode=pl.Buffered(1))


def _weight_specs(layer, win_spec):
    ls = lambda *shape: _layer_spec(shape, layer)
    return [ls(1, D_MODEL), ls(1, D_MODEL), ls(1, KV_WIDTH), ls(1, D_MODEL),
            win_spec, _const_spec((256, 256)), ls(4, POOL_GROUP, POOL_GROUP),
            ls(D_MODEL, D_MODEL), ls(D_MODEL, D_MODEL), ls(D_MODEL, D_MODEL), ls(D_MODEL, D_MODEL), ls(PLE_DIM, D_MODEL)]


def _weight_args(w, w_in):
    return [w["norm_g"], w["gq"], w["gk"], w["pool_scale"], w_in, w["hn"], w["pool_map"],
            w["w_pp"], w["w_pa"], w["w_out"], w["w_g"], w["w_ple"]]


def _prompt_layer(h, p, tabs, bias, w, layer):
    B, T, _ = h.shape
    nt = T // TQ
    nqb = TQ // WINDOW
    row = lambda width: pl.BlockSpec((1, TQ, width), lambda b, t: (b, t, 0))
    tab = pl.BlockSpec((TQ, LANES), lambda b, t: (t, 0))
    in_specs = [
        pl.BlockSpec(memory_space=pltpu.SMEM),
        row(D_MODEL), pl.BlockSpec((None, 1, TQ, PLE_DIM), lambda b, t: (layer, b, t, 0)), tab, tab, tab,
        _const_spec(bias.shape),
    ] + _weight_specs(layer, pl.BlockSpec(memory_space=pl.ANY))
    out_specs = [
        row(D_MODEL),
        pl.BlockSpec((1, WINDOW, KV_WIDTH), lambda b, t: (b, 0, 0)),
        pl.BlockSpec((1, WINDOW, KV_WIDTH), lambda b, t: (b, 0, 0)),
        pl.BlockSpec((1, POOL_STATE, D_MODEL), lambda b, t: (b, 0, 0)),
        pl.BlockSpec(memory_space=pl.ANY),
    ]
    out_shape = [
        jax.ShapeDtypeStruct((B, T, D_MODEL), F32),
        jax.ShapeDtypeStruct((B, WINDOW, KV_WIDTH), F32),
        jax.ShapeDtypeStruct((B, WINDOW, KV_WIDTH), F32),
        jax.ShapeDtypeStruct((B, POOL_STATE, D_MODEL), F32),
        jax.ShapeDtypeStruct((D_MODEL, C_END), BF16),
    ]
    scratch = [
        pltpu.VMEM((N_KV_HEADS, (nqb + 1) * 256, LANES), BF16),
        pltpu.VMEM((N_KV_HEADS, (nqb + 1) * 256, 2 * LANES), BF16),
        pltpu.VMEM((TQ + 16, D_MODEL), F32),
        pltpu.VMEM((TQ + 16, D_MODEL), F32),
        pltpu.VMEM((TQ + 16, 768), F32),
        pltpu.VMEM((TQ + 16, 512), F32),
        pltpu.VMEM((TQ, D_MODEL), F32),
        pltpu.VMEM((D_MODEL, C_END), BF16),
        pltpu.VMEM((2, WCH, C_END), F32),
        pltpu.SemaphoreType.DMA((3,)),
    ]
    return pl.pallas_call(
        functools.partial(_prompt_kernel, layer=layer),
        grid=(B, nt),
        in_specs=in_specs,
        out_specs=out_specs,
        out_shape=out_shape,
        scratch_shapes=scratch,
        compiler_params=pltpu.CompilerParams(dimension_semantics=("arbitrary", "arbitrary"),
                                             vmem_limit_bytes=VMEM_LIMIT),
        name="prompt_layer",
    )(w["sinks2"][layer], h, p, tabs[0], tabs[1], tabs[2], bias, *_weight_args(w, w["w_in_f32"]))


def _split3(x):
    hi = x.astype(BF16)
    r1 = x - hi.astype(F32)
    mid = r1.astype(BF16)
    return hi, mid, (r1 - mid.astype(F32)).astype(BF16)


def _sample_kernel(sinks_ref, h_ref, p_ref, st_ref, ck_ref, cv_ref, cos_ref, s1_ref, s2_ref, bmain_ref, bnew_ref, sel_ref,
                   ng_ref, gq_ref, gk_ref, ps_ref, win_ref, hn_ref, pm_ref,
                   wpp_ref, wpa_ref, wout_ref, wg_ref, wple_ref, *rest, nb, n_alias):
    y_ref, nkc_ref, nvc_ref, npool_ref, q_s, o_s, ya_s, nu_ref, nk_ref, nv_ref = rest[n_alias:]
    i = pl.program_id(0)
    nsteps = pl.num_programs(0)
    dec = h_ref.shape[0] // nb

    @pl.when(i == 0)
    def _():
        h = h_ref[...]
        xn = _rms_in(h, ng_ref[...])

        def proj(c0, c1):
            return _dot(xn, win_ref[:, c0:c1])

        nu_ref[...] = proj(C_U, C_ZP)
        zp = proj(C_ZP, C_Q)
        ya_s[...] = zp * _sigmoid(zp)

        cos, s1, s2t = cos_ref[...], s1_ref[...], s2_ref[...]
        hn = hn_ref[...]
        q_s[...] = _rope(_head_norm(proj(C_Q, C_K), hn, gq_ref[...]), cos, s1, s2t)
        nk_ref[...] = _rope(_head_norm(proj(C_K, C_V), hn, gk_ref[...]), cos, s1, s2t)
        nv_ref[...] = proj(C_V, C_ZA)

    b0 = pl.multiple_of(i * SB, SB)
    rpp = dec * SB
    nrow = N_HEADS * rpp
    lo = lax.broadcasted_iota(jnp.int32, (rpp, LANES), 1) < HEAD_DIM
    zero = jnp.zeros((rpp, LANES), F32)

    def rows_tb(ref, c0, c1):
        return jnp.concatenate([ref[pl.ds(pl.multiple_of(tt * nb + b0, SB), SB), c0:c1] for tt in range(dec)], axis=0)

    u_tb = rows_tb(nu_ref, 0, D_MODEL)
    seq = [st_ref[j] for j in range(POOL_STATE)] + [u_tb[tt * SB:(tt + 1) * SB] for tt in range(dec)]
    r_rows = []
    for tt in range(dec):
        e = POOL_STATE + tt
        cols = []
        for g, w in enumerate(POOL_WINDOWS):
            cs = slice(g * POOL_GROUP, (g + 1) * POOL_GROUP)
            acc = seq[e][:, cs]
            for d in range(1, w):
                acc = acc + seq[e - d][:, cs]
            cols.append(acc * (1.0 / w))
        r_rows.append(jnp.concatenate(cols, axis=1) - seq[e])
    ya_tb = _pool_map(jnp.concatenate(r_rows, axis=0), pm_ref, ps_ref[...])
    for tt in range(dec):
        rs = pl.ds(pl.multiple_of(tt * nb + b0, SB), SB)
        ya_s[rs, :] = ya_s[rs, :] * ya_tb[tt * SB:(tt + 1) * SB]

    slot = lax.broadcasted_iota(jnp.int32, (KV_WIDTH, WINDOW), 1)
    knew_f = rows_tb(nk_ref, 0, KV_WIDTH)
    vnew_f = rows_tb(nv_ref, 0, KV_WIDTH)

    def new_cache(src_ref, new_f, dst_ref):
        pieces3 = _split3(new_f)
        for bl in range(SB):
            cols = None
            for pc in pieces3:
                part = lax.dot_general(pc, sel_ref[bl], (((0,), (0,)), ((), ())), preferred_element_type=F32)
                cols = part if cols is None else cols + part
            dst_ref[bl] = jnp.where(slot < WINDOW - dec, pltpu.roll(src_ref[bl], WINDOW - dec, 1), cols)

    pieces = []
    for kh in range(N_KV_HEADS):
        for gp in range(GQA_GROUP // 2):
            c = kh * 2 + gp
            x = rows_tb(q_s, c * LANES, (c + 1) * LANES)
            rx = pltpu.roll(x, HEAD_DIM, 1)
            for e in range(2):
                src = x if e == kh % 2 else rx
                half = jnp.where(lo, src, 0.0) if kh % 2 == 0 else jnp.where(lo, 0.0, src)
                pieces.append(jnp.concatenate([half, zero] if kh < 2 else [zero, half], axis=1))
    lhs = jnp.concatenate(pieces, axis=0).astype(BF16)
    kmain = jnp.concatenate([ck_ref[bl].astype(BF16) for bl in range(SB)], axis=1)
    vmain = jnp.concatenate([cv_ref[bl].astype(BF16) for bl in range(SB)], axis=1)
    knew = knew_f.astype(BF16)
    vnew = vnew_f.astype(BF16)
    s_main = (_dot(lhs, kmain).reshape(N_HEADS, rpp, SB * WINDOW) + bmain_ref[...][None]).reshape(nrow, SB * WINDOW)
    s_new = (_dot_t(lhs, knew).reshape(N_HEADS, rpp, rpp) + bnew_ref[...][None]).reshape(nrow, rpp)

    new_cache(ck_ref, knew_f, nkc_ref)
    new_cache(cv_ref, vnew_f, nvc_ref)

    sink = jnp.concatenate([jnp.full((rpp, 1), sinks_ref[hh], F32) for hh in range(N_HEADS)], axis=0)
    mx = jnp.maximum(jnp.maximum(jnp.max(s_main, axis=-1, keepdims=True), jnp.max(s_new, axis=-1, keepdims=True)), sink)
    p_main = jnp.exp2(s_main - mx).astype(BF16)
    p_new = jnp.exp2(s_new - mx).astype(BF16)
    den = (jnp.sum(p_main.astype(F32), axis=-1, keepdims=True) + jnp.sum(p_new.astype(F32), axis=-1, keepdims=True)
           + jnp.exp2(sink - mx))
    o = (_dot_t(p_main, vmain) + _dot(p_new, vnew)) / den
    for kh in range(N_KV_HEADS):
        for gp in range(GQA_GROUP // 2):
            halves = []
            for e in range(2):
                hh = kh * GQA_GROUP + 2 * gp + e
                x = o[hh * rpp:(hh + 1) * rpp, (kh // 2) * LANES:(kh // 2 + 1) * LANES]
                halves.append(x if e == kh % 2 else pltpu.roll(x, HEAD_DIM, 1))
            dest = jnp.where(lo, halves[0], halves[1])
            c = kh * 2 + gp
            for tt in range(dec):
                o_s[pl.ds(pl.multiple_of(tt * nb + b0, SB), SB), c * LANES:(c + 1) * LANES] = dest[tt * SB:(tt + 1) * SB]

    for j in range(POOL_STATE):
        npool_ref[j] = seq[j + dec]

    @pl.when(i == nsteps - 1)
    def _():
        h = h_ref[...]
        xn = _rms_in(h, ng_ref[...])
        za = _dot(xn, win_ref[:, C_ZA:C_GA])
        yb = o_s[...] * (za * _sigmoid(za))
        y_ref[...] = _tail(h, p_ref[...], ya_s[...], yb, _dot(xn, win_ref[:, C_GA:C_GB]), _dot(xn, win_ref[:, C_GB:C_END]),
                           wpp_ref[...], wpa_ref[...], wout_ref[...], wg_ref[...], wple_ref[...])


def _sample_layer(h, p, st, ck, cv, tabs, bmain, bnew, sel, w, w_in_bf, layer, stacked):
    R = h.shape[0]
    nb = ck.shape[1]
    full = lambda shape: pl.BlockSpec(shape, lambda i: (0,) * len(shape), pipeline_mode=pl.Buffered(1))
    cache = pl.BlockSpec((None, SB, KV_WIDTH, WINDOW), lambda i: (layer, i, 0, 0))
    state = pl.BlockSpec((None, POOL_STATE, SB, D_MODEL), lambda i: (layer, 0, i, 0))
    in_specs = [
        pl.BlockSpec(memory_space=pltpu.SMEM),
        full((R, D_MODEL)), full((R, PLE_DIM)), state, cache, cache,
        full((R, LANES)), full((R, LANES)), full((R, LANES)), full(bmain.shape), full(bnew.shape), full(sel.shape),
    ] + _weight_specs(layer, full((D_MODEL, C_END)))
    out_specs = [full((R, D_MODEL)), cache, cache, state]
    out_shape = [jax.ShapeDtypeStruct((R, D_MODEL), F32), jax.ShapeDtypeStruct(ck.shape, F32),
                 jax.ShapeDtypeStruct(cv.shape, F32), jax.ShapeDtypeStruct(st.shape, F32)]
    scratch = [pltpu.VMEM((R, D_MODEL), F32) for _ in range(4)] + [pltpu.VMEM((R, KV_WIDTH), F32) for _ in range(2)]
    args = [w["sinks2"][layer], h, p, st, ck, cv, tabs[0], tabs[1], tabs[2], bmain, bnew, sel] + _weight_args(w, w_in_bf)
    aliases = {}
    if stacked is not None:
        aliases = {len(args) + k: 1 + k for k in range(len(stacked))}
        args += list(stacked)
        in_specs += [pl.BlockSpec(memory_space=pl.ANY)] * len(stacked)
    return pl.pallas_call(
        functools.partial(_sample_kernel, nb=nb, n_alias=len(aliases)),
        grid=(nb // SB,),
        in_specs=in_specs,
        out_specs=out_specs,
        out_shape=out_shape,
        scratch_shapes=scratch,
        input_output_aliases=aliases,
        compiler_params=pltpu.CompilerParams(dimension_semantics=("arbitrary",), vmem_limit_bytes=VMEM_LIMIT),
        name="sample_layer",
    )(*args)


def _rope_tables(pos):
    pos = np.asarray(pos, np.float64)
    inv_freq = ROPE_THETA ** (-np.arange(0, ROPE_DIM, 2, dtype=np.float64) / ROPE_DIM)
    ang = pos[:, None] * inv_freq[None, :]
    cos, sin = np.cos(ang), np.sin(ang)
    n = pos.shape[0]
    half = ROPE_DIM // 2
    c64 = np.concatenate([cos, cos, np.ones((n, HEAD_DIM - ROPE_DIM))], axis=1)
    a64 = np.concatenate([-sin, np.zeros((n, HEAD_DIM - half))], axis=1)
    b64 = np.concatenate([np.zeros((n, half)), sin, np.zeros((n, HEAD_DIM - ROPE_DIM))], axis=1)
    rep = LANES // HEAD_DIM
    return tuple(jnp.asarray(np.tile(x, (1, rep)), F32) for x in (c64, a64, b64))


def _prompt_bias():
    q = np.arange(WINDOW)[:, None]
    j = np.arange(WINDOW)[None, :]
    prev = np.where(j >= q, 0.0, NEG)
    cur = np.where(j <= q, 0.0, NEG)
    normal = np.concatenate([prev, cur], axis=1)
    first = np.concatenate([np.full_like(prev, NEG), cur], axis=1)
    return jnp.asarray(np.stack([normal, first]), F32)


def _sample_bias(dec):
    rq = np.arange(dec * SB)
    tq, bq = rq // SB, rq % SB
    cm = np.arange(SB * WINDOW)
    bm, jm = cm // WINDOW, cm % WINDOW
    main = np.where((bq[:, None] == bm[None, :]) & (jm[None, :] >= tq[:, None]), 0.0, NEG)
    new = np.where((bq[:, None] == bq[None, :]) & (tq[None, :] <= tq[:, None]), 0.0, NEG)
    sel = np.zeros((SB, dec * SB, WINDOW))
    for r in rq:
        sel[bq[r], r, WINDOW - dec + tq[r]] = 1.0
    return jnp.asarray(main, F32), jnp.asarray(new, F32), jnp.asarray(sel, BF16)


def _prep_weights(norm_g, w_in, q_norm_g, k_norm_g, sinks, pool_map, pool_scale,
                  w_proj_pool, w_proj_attn, w_out, w_ple, w_ple_gate):
    qscale = HEAD_DIM ** -0.5 * LOG2E
    blk = np.kron(np.eye(256 // HEAD_DIM), np.full((HEAD_DIM, HEAD_DIM), 1.0 / HEAD_DIM))
    return dict(
        sinks2=sinks * LOG2E,
        norm_g=norm_g[:, None, :],
        gq=jnp.tile(q_norm_g * qscale, (1, N_HEADS))[:, None, :],
        gk=jnp.tile(k_norm_g, (1, N_KV_HEADS))[:, None, :],
        pool_scale=pool_scale[:, None, :],
        w_in_f32=w_in,
        hn=jnp.asarray(blk, BF16),
        pool_map=pool_map.astype(BF16),
        w_pp=w_proj_pool.astype(BF16), w_pa=w_proj_attn.astype(BF16),
        w_out=w_out.astype(BF16), w_g=w_ple_gate.astype(BF16), w_ple=w_ple.astype(BF16),
    )


def kernel(x_prompt, x_sample, cache_k, cache_v, state_pool, p_prompt, p_sample, norm_g, w_in, q_norm_g, k_norm_g, sinks, pool_map, pool_scale, w_proj_pool, w_proj_attn, w_out, w_ple, w_ple_gate):
    depth = w_in.shape[0]
    B, T, _ = x_prompt.shape
    nb, dec, _ = x_sample.shape
    assert T % TQ == 0 and TQ % WINDOW == 0 and nb % SB == 0
    ptabs = _rope_tables(np.arange(T))
    stabs = _rope_tables(np.repeat(PAST_LEN + np.arange(dec), nb))
    pbias = _prompt_bias()
    bmain, bnew, sel = _sample_bias(dec)

    hp = x_prompt
    hs = x_sample.transpose(1, 0, 2).reshape(dec * nb, D_MODEL)
    ck_all = cache_k.transpose(0, 1, 3, 4, 2).reshape(depth, nb, KV_WIDTH, WINDOW)
    cv_all = cache_v.transpose(0, 1, 3, 4, 2).reshape(depth, nb, KV_WIDTH, WINDOW)
    st_all = state_pool.transpose(0, 2, 1, 3)
    stacked = None
    kp_l, vp_l, pp_l = [], [], []
    w = _prep_weights(norm_g, w_in, q_norm_g, k_norm_g, sinks, pool_map, pool_scale,
                      w_proj_pool, w_proj_attn, w_out, w_ple, w_ple_gate)
    for i in range(depth):
        hp, nk, nv, npool, w_in_bf = _prompt_layer(hp, p_prompt, ptabs, pbias, w, i)
        kp_l.append(nk.reshape(B, WINDOW, N_KV_HEADS, HEAD_DIM))
        vp_l.append(nv.reshape(B, WINDOW, N_KV_HEADS, HEAD_DIM))
        pp_l.append(npool)

        ps_t = p_sample[i].transpose(1, 0, 2).reshape(dec * nb, PLE_DIM)
        hs, *stacked = _sample_layer(hs, ps_t, st_all, ck_all, cv_all, stabs, bmain, bnew, sel, w, w_in_bf, i, stacked)
    uncache = lambda a: a.reshape(depth, nb, N_KV_HEADS, HEAD_DIM, WINDOW).transpose(0, 1, 4, 2, 3)
    new_k_sample, new_v_sample = uncache(stacked[0]), uncache(stacked[1])
    new_pool_sample = stacked[2].transpose(0, 2, 1, 3)
    y_sample = hs.reshape(dec, nb, D_MODEL).transpose(1, 0, 2)
    return (hp, y_sample, jnp.stack(kp_l), jnp.stack(vp_l), jnp.stack(pp_l),
            new_k_sample, new_v_sample, new_pool_sample)
```

```python
import functools
import math

import numpy as np
import jax
import jax.numpy as jnp
from jax import lax
from jax.experimental import pallas as pl
from jax.experimental.pallas import tpu as pltpu

D_MODEL = 1024
PLE_DIM = 256
POOL_WINDOWS = (2, 4, 8, 16)
POOL_GROUP = D_MODEL // len(POOL_WINDOWS)
POOL_STATE = max(POOL_WINDOWS) - 1
HEAD_DIM = 64
N_HEADS = D_MODEL // HEAD_DIM
N_KV_HEADS = 4
GQA_GROUP = N_HEADS // N_KV_HEADS
KV_WIDTH = N_KV_HEADS * HEAD_DIM
WINDOW = 128
ROPE_THETA = 500000.0
ROPE_DIM = HEAD_DIM // 4
EPS = 1e-6
PAST_LEN = 16384

C_U, C_ZP, C_Q, C_K, C_V, C_ZA, C_GA, C_GB, C_END = 0, 1024, 2048, 3072, 3328, 3584, 4608, 5632, 6656

LANES = 128
SUBLANES = 8
NEG = -1e30
LOG2E = math.log2(math.e)
VMEM_LIMIT = 60 * 1024 * 1024

TQ = 512
FCH = 256
WCH = 32
WSLOTS = 4
SB = 8

F32 = jnp.float32
BF16 = jnp.bfloat16


def _sigmoid(x):
    return 1.0 / (1.0 + jnp.exp2(x * (-LOG2E)))


def _dot(a, b):
    return jnp.dot(a, b, preferred_element_type=F32)


def _dot_t(a, b):
    return lax.dot_general(a, b, (((1,), (1,)), ((), ())), preferred_element_type=F32)


def _rope(x, cos, s1, s2):
    outs = []
    for c in range(x.shape[1] // LANES):
        xc = x[:, c * LANES:(c + 1) * LANES]
        outs.append(xc * cos + pltpu.roll(xc, LANES - ROPE_DIM // 2, 1) * s1 + pltpu.roll(xc, ROPE_DIM // 2, 1) * s2)
    return outs[0] if len(outs) == 1 else jnp.concatenate(outs, axis=1)


def _head_norm(x, hn, gain):
    sq = (x * x).astype(BF16)
    ms = jnp.concatenate([_dot(sq[:, c * 256:(c + 1) * 256], hn) for c in range(x.shape[1] // 256)], axis=1)
    return x * lax.rsqrt(ms + EPS) * gain


def _rms_in(h, g):
    ms = jnp.mean(h * h, axis=-1, keepdims=True)
    return (h * lax.rsqrt(ms + EPS) * g).astype(BF16)


def _tail(h, p, ya, yb, ga, gb, wpp, wpa, wout, wg, wple):
    m = _sigmoid(ga) * _dot(ya.astype(BF16), wpp) + _sigmoid(gb) * _dot(yb.astype(BF16), wpa)
    h1 = h + _dot(m.astype(BF16), wout)
    gate = _sigmoid(_dot(h1.astype(BF16), wg))
    return h1 + gate * _dot(p.astype(BF16), wple)


def _pool_map(r, pm_ref, ps):
    rb = r.astype(BF16)
    mapped = jnp.concatenate([_dot(rb[:, g * POOL_GROUP:(g + 1) * POOL_GROUP], pm_ref[g])
                              for g in range(len(POOL_WINDOWS))], axis=1)
    return mapped * ps


def _prompt_kernel(sinks_ref, h_ref, p_ref, cos_ref, s1_ref, s2_ref, bias_ref,
                   ng_ref, gq_ref, gk_ref, ps_ref, winf_hbm, hn_ref, pm_ref,
                   wpp_ref, wpa_ref, wout_ref, wg_ref, wple_ref,
                   y_ref, nk_ref, nv_ref, npool_ref, winb_hbm,
                   kall, vall, u_s, s2_s, s4_s, s8_s, o_s, win_ref, stg, sem, *, layer):
    t = pl.program_id(1)
    first_step = jnp.logical_and(pl.program_id(0) == 0, t == 0)
    last_step = jnp.logical_and(pl.program_id(0) == pl.num_programs(0) - 1, t == pl.num_programs(1) - 1)
    nqb = TQ // WINDOW

    def stage(c):
        slot = c % WSLOTS
        return pltpu.make_async_copy(winf_hbm.at[layer, pl.ds(c * WCH, WCH), :], stg.at[slot], sem.at[slot])

    publish = pltpu.make_async_copy(win_ref, winb_hbm, sem.at[WSLOTS])

    @pl.when(first_step)
    def _():
        nch = D_MODEL // WCH
        for c in range(WSLOTS - 1):
            stage(c).start()
        for c in range(nch):
            if c + WSLOTS - 1 < nch:
                stage(c + WSLOTS - 1).start()
            stage(c).wait()
            win_ref[c * WCH:(c + 1) * WCH, :] = stg[c % WSLOTS].astype(BF16)
        publish.start()
    lane = lax.broadcasted_iota(jnp.int32, (WINDOW, LANES), 1)
    lo = lane < HEAD_DIM

    @pl.when(t == 0)
    def _():
        kall[...] = jnp.zeros_like(kall)
        vall[:, :, 0:LANES] = jnp.zeros((N_KV_HEADS, (nqb + 1) * 256, LANES), BF16)
        ones_pat = jnp.concatenate([jnp.where(lo, 1.0, 0.0), jnp.where(lo, 0.0, 1.0)], axis=0).astype(BF16)
        for kh in range(N_KV_HEADS):
            for x in range(nqb + 1):
                vall[kh, x * 256:(x + 1) * 256, LANES:2 * LANES] = ones_pat
        u_s[0:16, :] = jnp.zeros((16, D_MODEL), F32)
        s2_s[0:16, :] = jnp.zeros((16, D_MODEL), F32)
        s4_s[0:16, :] = jnp.zeros((16, 768), F32)
        s8_s[0:16, :] = jnp.zeros((16, 512), F32)

    @pl.when(t > 0)
    def _():
        for kh in range(N_KV_HEADS):
            kall[kh, 0:256, :] = kall[kh, nqb * 256:(nqb + 1) * 256, :]
            vall[kh, 0:256, 0:LANES] = vall[kh, nqb * 256:(nqb + 1) * 256, 0:LANES]

    h = h_ref[0]
    xn = _rms_in(h, ng_ref[...])

    def proj(c0, c1):
        return _dot(xn, win_ref[:, c0:c1])

    res = {}

    def fillers():
        parts = []
        for c in range(D_MODEL // FCH):
            parts.append(proj(C_U + c * FCH, C_U + (c + 1) * FCH))
            yield
        u = jnp.concatenate(parts, axis=1)
        u_s[16:16 + TQ, :] = u
        s2 = u + u_s[15:15 + TQ, :]
        s2_s[16:16 + TQ, :] = s2
        s4 = s2[:, 256:] + s2_s[14:14 + TQ, 256:]
        s4_s[16:16 + TQ, :] = s4
        s8 = s4[:, 256:] + s4_s[12:12 + TQ, 256:]
        s8_s[16:16 + TQ, :] = s8
        s16 = s8[:, 256:] + s8_s[8:8 + TQ, 256:]
        pos1 = (t * TQ + 1 + lax.broadcasted_iota(jnp.int32, (TQ, 1), 0)).astype(F32)
        wins = (s2[:, :256], s4[:, :256], s8[:, :256], s16)
        r = jnp.concatenate([wins[g] * (1.0 / jnp.minimum(pos1, float(w))) for g, w in enumerate(POOL_WINDOWS)],
                            axis=1) - u
        npool_ref[0] = u_s[TQ + 1:TQ + 16, :]
        u_s[0:16, :] = u_s[TQ:TQ + 16, :]
        s2_s[0:16, :] = s2_s[TQ:TQ + 16, :]
        s4_s[0:16, :] = s4_s[TQ:TQ + 16, :]
        s8_s[0:16, :] = s8_s[TQ:TQ + 16, :]
        mapped = _pool_map(r, pm_ref, ps_ref[...])
        yield
        for name, c_lo in (("zp", C_ZP), ("za", C_ZA), ("ga", C_GA), ("gb", C_GB)):
            parts = []
            for c in range(D_MODEL // FCH):
                parts.append(proj(c_lo + c * FCH, c_lo + (c + 1) * FCH))
                yield
            res[name] = jnp.concatenate(parts, axis=1)
        yab = (mapped * (res["zp"] * _sigmoid(res["zp"]))).astype(BF16)
        pb = p_ref[0].astype(BF16)
        for name, lhs, w_ref in (("pp", yab, wpp_ref), ("ple", pb, wple_ref)):
            parts = []
            for c in range(D_MODEL // FCH):
                parts.append(_dot(lhs, w_ref[:, c * FCH:(c + 1) * FCH]))
                yield
            res[name] = jnp.concatenate(parts, axis=1)

    def tiled_bias(b):
        b4 = jnp.concatenate([b[:, :LANES], b[:, :LANES], b[:, LANES:], b[:, LANES:]], axis=1)
        return jnp.concatenate([b4, b4], axis=0)

    biases = [tiled_bias(jnp.where(t == 0, bias_ref[1], bias_ref[0]))] + [tiled_bias(bias_ref[0])] * (nqb - 1)
    chains = [(kh, n) for kh in range(N_KV_HEADS) for n in range(nqb)]
    n_fill = 7 * (D_MODEL // FCH) + 1
    fill = fillers()

    def scores(kh, n):
        rows = slice(n * WINDOW, (n + 1) * WINDOW)
        qs = jnp.concatenate([qb[rows, kh * 256:kh * 256 + LANES],
                              qb[rows, kh * 256 + LANES:(kh + 1) * 256]], axis=0)
        return _dot_t(qs, kall[kh, n * 256:n * 256 + 512, :]) + biases[n]

    q_raw, k_raw, v = proj(C_Q, C_K), proj(C_K, C_V), proj(C_V, C_ZA)
    pre = D_MODEL // FCH + 1
    for _ in range(pre):
        next(fill)
    done = pre
    cos, s1, s2t = cos_ref[...], s1_ref[...], s2_ref[...]
    hn = hn_ref[...]
    kr = _rope(_head_norm(k_raw, hn, gk_ref[...]), cos, s1, s2t)
    qb = _rope(_head_norm(q_raw, hn, gq_ref[...]), cos, s1, s2t).astype(BF16)
    nk_ref[0] = kr[TQ - WINDOW:, :]
    nv_ref[0] = v[TQ - WINDOW:, :]
    for x in range(nqb):
        rows = slice(x * WINDOW, (x + 1) * WINDOW)
        base = (x + 1) * 256
        for pr in range(N_KV_HEADS // 2):
            for src, dst, col in ((kr, kall, None), (v, vall, slice(0, LANES))):
                a = src[rows, pr * LANES:(pr + 1) * LANES]
                ra = pltpu.roll(a, HEAD_DIM, 1)
                parts = ((jnp.where(lo, a, 0.0), jnp.where(lo, 0.0, ra)),
                         (jnp.where(lo, ra, 0.0), jnp.where(lo, 0.0, a)))
                for e in range(2):
                    kh = 2 * pr + e
                    lh = jnp.concatenate(parts[e], axis=0).astype(BF16)
                    if col is None:
                        dst[kh, base:base + 256, :] = lh
                    else:
                        dst[kh, base:base + 256, col] = lh

    s_next = scores(*chains[0])
    for ci, (kh, n) in enumerate(chains):
        s = s_next
        if ci + 1 < len(chains):
            s_next = scores(*chains[ci + 1])
        want = pre + ((n_fill - pre) * (ci + 1)) // len(chains)
        while done < want:
            next(fill)
            done += 1
        rows = slice(n * WINDOW, (n + 1) * WINDOW)
        p_rows, sink_rows = [], []
        for rr in range(2):
            rs = slice(rr * WINDOW, (rr + 1) * WINDOW)
            pcols = [None] * 4
            sterm = []
            for e in range(2):
                sink = sinks_ref[kh * GQA_GROUP + 2 * rr + e]
                sp = s[rs, e * LANES:(e + 1) * LANES]
                sc = s[rs, 256 + e * LANES:256 + (e + 1) * LANES]
                mx = jnp.maximum(jnp.max(jnp.maximum(sp, sc), axis=-1, keepdims=True), sink)
                pcols[e] = jnp.exp2(sp - mx)
                pcols[2 + e] = jnp.exp2(sc - mx)
                sterm.append(jnp.exp2(sink - mx))
            p_rows.append(jnp.concatenate(pcols, axis=1))
            sink_rows.append(jnp.where(lo, sterm[0], sterm[1]))
        pmat = jnp.concatenate(p_rows, axis=0).astype(BF16)
        o2 = _dot(pmat, vall[kh, n * 256:n * 256 + 512, :])
        o = o2[:, :LANES] / (o2[:, LANES:] + jnp.concatenate(sink_rows, axis=0))
        o_s[rows, kh * 256:kh * 256 + LANES] = o[:WINDOW]
        o_s[rows, kh * 256 + LANES:(kh + 1) * 256] = o[WINDOW:]
    for _ in fill:
        pass

    za, ga, gb = res["za"], res["ga"], res["gb"]
    yb = o_s[...] * (za * _sigmoid(za))
    m = _sigmoid(ga) * res["pp"] + _sigmoid(gb) * _dot(yb.astype(BF16), wpa_ref[...])
    h1 = h + _dot(m.astype(BF16), wout_ref[...])
    gate = _sigmoid(_dot(h1.astype(BF16), wg_ref[...]))
    y_ref[0] = h1 + gate * res["ple"]

    @pl.when(last_step)
    def _():
        publish.wait()


def _const_spec(shape):
    nd = len(shape)
    return pl.BlockSpec(shape, lambda *_: (0,) * nd, pipeline_mode=pl.Buffered(1))


def _layer_spec(shape, layer):
    nd = len(shape)
    return pl.BlockSpec((None,) + tuple(shape), lambda *_: (layer,) + (0,) * nd, pipeline_mode=pl.Buffered(1))


def _weight_specs(layer, win_spec):
    ls = lambda *shape: _layer_spec(shape, layer)
    return [ls(1, D_MODEL), ls(1, D_MODEL), ls(1, KV_WIDTH), ls(1, D_MODEL),
            win_spec, _const_spec((256, 256)), ls(4, POOL_GROUP, POOL_GROUP),
            ls(D_MODEL, D_MODEL), ls(D_MODEL, D_MODEL), ls(D_MODEL, D_MODEL), ls(D_MODEL, D_MODEL), ls(PLE_DIM, D_MODEL)]


def _weight_args(w, w_in):
    return [w["norm_g"], w["gq"], w["gk"], w["pool_scale"], w_in, w["hn"], w["pool_map"],
            w["w_pp"], w["w_pa"], w["w_out"], w["w_g"], w["w_ple"]]


def _prompt_layer(h, p, tabs, bias, w, layer):
    B, T, _ = h.shape
    nt = T // TQ
    nqb = TQ // WINDOW
    row = lambda width: pl.BlockSpec((1, TQ, width), lambda b, t: (b, t, 0))
    tab = pl.BlockSpec((TQ, LANES), lambda b, t: (t, 0))
    in_specs = [
        pl.BlockSpec(memory_space=pltpu.SMEM),
        row(D_MODEL), pl.BlockSpec((None, 1, TQ, PLE_DIM), lambda b, t: (layer, b, t, 0)), tab, tab, tab,
        _const_spec(bias.shape),
    ] + _weight_specs(layer, pl.BlockSpec(memory_space=pl.ANY))
    out_specs = [
        row(D_MODEL),
        pl.BlockSpec((1, WINDOW, KV_WIDTH), lambda b, t: (b, 0, 0)),
        pl.BlockSpec((1, WINDOW, KV_WIDTH), lambda b, t: (b, 0, 0)),
        pl.BlockSpec((1, POOL_STATE, D_MODEL), lambda b, t: (b, 0, 0)),
        pl.BlockSpec(memory_space=pl.ANY),
    ]
    out_shape = [
        jax.ShapeDtypeStruct((B, T, D_MODEL), F32),
        jax.ShapeDtypeStruct((B, WINDOW, KV_WIDTH), F32),
        jax.ShapeDtypeStruct((B, WINDOW, KV_WIDTH), F32),
        jax.ShapeDtypeStruct((B, POOL_STATE, D_MODEL), F32),
        jax.ShapeDtypeStruct((D_MODEL, C_END), BF16),
    ]
    scratch = [
        pltpu.VMEM((N_KV_HEADS, (nqb + 1) * 256, LANES), BF16),
        pltpu.VMEM((N_KV_HEADS, (nqb + 1) * 256, 2 * LANES), BF16),
        pltpu.VMEM((TQ + 16, D_MODEL), F32),
        pltpu.VMEM((TQ + 16, D_MODEL), F32),
        pltpu.VMEM((TQ + 16, 768), F32),
        pltpu.VMEM((TQ + 16, 512), F32),
        pltpu.VMEM((TQ, D_MODEL), F32),
        pltpu.VMEM((D_MODEL, C_END), BF16),
        pltpu.VMEM((WSLOTS, WCH, C_END), F32),
        pltpu.SemaphoreType.DMA((WSLOTS + 1,)),
    ]
    return pl.pallas_call(
        functools.partial(_prompt_kernel, layer=layer),
        grid=(B, nt),
        in_specs=in_specs,
        out_specs=out_specs,
        out_shape=out_shape,
        scratch_shapes=scratch,
        compiler_params=pltpu.CompilerParams(dimension_semantics=("arbitrary", "arbitrary"),
                                             vmem_limit_bytes=VMEM_LIMIT),
        name="prompt_layer",
    )(w["sinks2"][layer], h, p, tabs[0], tabs[1], tabs[2], bias, *_weight_args(w, w["w_in_f32"]))


def _split3(x):
    hi = x.astype(BF16)
    r1 = x - hi.astype(F32)
    mid = r1.astype(BF16)
    return hi, mid, (r1 - mid.astype(F32)).astype(BF16)


def _sample_kernel(sinks_ref, h_ref, p_ref, st_ref, ck_ref, cv_ref, cos_ref, s1_ref, s2_ref, bmain_ref, bnew_ref, sel_ref,
                   ng_ref, gq_ref, gk_ref, ps_ref, win_ref, hn_ref, pm_ref,
                   wpp_ref, wpa_ref, wout_ref, wg_ref, wple_ref, *rest, nb, n_alias):
    y_ref, nkc_ref, nvc_ref, npool_ref, q_s, o_s, ya_s, nu_ref, nk_ref, nv_ref = rest[n_alias:]
    i = pl.program_id(0)
    nsteps = pl.num_programs(0)
    dec = h_ref.shape[0] // nb

    @pl.when(i == 0)
    def _():
        h = h_ref[...]
        xn = _rms_in(h, ng_ref[...])

        def proj(c0, c1):
            return _dot(xn, win_ref[:, c0:c1])

        nu_ref[...] = proj(C_U, C_ZP)
        zp = proj(C_ZP, C_Q)
        ya_s[...] = zp * _sigmoid(zp)

        cos, s1, s2t = cos_ref[...], s1_ref[...], s2_ref[...]
        hn = hn_ref[...]
        q_s[...] = _rope(_head_norm(proj(C_Q, C_K), hn, gq_ref[...]), cos, s1, s2t)
        nk_ref[...] = _rope(_head_norm(proj(C_K, C_V), hn, gk_ref[...]), cos, s1, s2t)
        nv_ref[...] = proj(C_V, C_ZA)

    b0 = pl.multiple_of(i * SB, SB)
    rpp = dec * SB
    nrow = N_HEADS * rpp
    lo = lax.broadcasted_iota(jnp.int32, (rpp, LANES), 1) < HEAD_DIM
    zero = jnp.zeros((rpp, LANES), F32)

    def rows_tb(ref, c0, c1):
        return jnp.concatenate([ref[pl.ds(pl.multiple_of(tt * nb + b0, SB), SB), c0:c1] for tt in range(dec)], axis=0)

    u_tb = rows_tb(nu_ref, 0, D_MODEL)
    seq = [st_ref[j] for j in range(POOL_STATE)] + [u_tb[tt * SB:(tt + 1) * SB] for tt in range(dec)]
    r_rows = []
    for tt in range(dec):
        e = POOL_STATE + tt
        cols = []
        for g, w in enumerate(POOL_WINDOWS):
            cs = slice(g * POOL_GROUP, (g + 1) * POOL_GROUP)
            acc = seq[e][:, cs]
            for d in range(1, w):
                acc = acc + seq[e - d][:, cs]
            cols.append(acc * (1.0 / w))
        r_rows.append(jnp.concatenate(cols, axis=1) - seq[e])
    ya_tb = _pool_map(jnp.concatenate(r_rows, axis=0), pm_ref, ps_ref[...])
    for tt in range(dec):
        rs = pl.ds(pl.multiple_of(tt * nb + b0, SB), SB)
        ya_s[rs, :] = ya_s[rs, :] * ya_tb[tt * SB:(tt + 1) * SB]

    slot = lax.broadcasted_iota(jnp.int32, (KV_WIDTH, WINDOW), 1)
    knew_f = rows_tb(nk_ref, 0, KV_WIDTH)
    vnew_f = rows_tb(nv_ref, 0, KV_WIDTH)

    def new_cache(src_ref, new_f, dst_ref):
        pieces3 = _split3(new_f)
        for bl in range(SB):
            cols = None
            for pc in pieces3:
                part = lax.dot_general(pc, sel_ref[bl], (((0,), (0,)), ((), ())), preferred_element_type=F32)
                cols = part if cols is None else cols + part
            dst_ref[bl] = jnp.where(slot < WINDOW - dec, pltpu.roll(src_ref[bl], WINDOW - dec, 1), cols)

    pieces = []
    for kh in range(N_KV_HEADS):
        for gp in range(GQA_GROUP // 2):
            c = kh * 2 + gp
            x = rows_tb(q_s, c * LANES, (c + 1) * LANES)
            rx = pltpu.roll(x, HEAD_DIM, 1)
            for e in range(2):
                src = x if e == kh % 2 else rx
                half = jnp.where(lo, src, 0.0) if kh % 2 == 0 else jnp.where(lo, 0.0, src)
                pieces.append(jnp.concatenate([half, zero] if kh < 2 else [zero, half], axis=1))
    lhs = jnp.concatenate(pieces, axis=0).astype(BF16)
    kmain = jnp.concatenate([ck_ref[bl].astype(BF16) for bl in range(SB)], axis=1)
    vmain = jnp.concatenate([cv_ref[bl].astype(BF16) for bl in range(SB)], axis=1)
    knew = knew_f.astype(BF16)
    vnew = vnew_f.astype(BF16)
    s_main = (_dot(lhs, kmain).reshape(N_HEADS, rpp, SB * WINDOW) + bmain_ref[...][None]).reshape(nrow, SB * WINDOW)
    s_new = (_dot_t(lhs, knew).reshape(N_HEADS, rpp, rpp) + bnew_ref[...][None]).reshape(nrow, rpp)

    new_cache(ck_ref, knew_f, nkc_ref)
    new_cache(cv_ref, vnew_f, nvc_ref)

    sink = jnp.concatenate([jnp.full((rpp, 1), sinks_ref[hh], F32) for hh in range(N_HEADS)], axis=0)
    mx = jnp.maximum(jnp.maximum(jnp.max(s_main, axis=-1, keepdims=True), jnp.max(s_new, axis=-1, keepdims=True)), sink)
    p_main = jnp.exp2(s_main - mx).astype(BF16)
    p_new = jnp.exp2(s_new - mx).astype(BF16)
    den = (jnp.sum(p_main.astype(F32), axis=-1, keepdims=True) + jnp.sum(p_new.astype(F32), axis=-1, keepdims=True)
           + jnp.exp2(sink - mx))
    o = (_dot_t(p_main, vmain) + _dot(p_new, vnew)) / den
    for kh in range(N_KV_HEADS):
        for gp in range(GQA_GROUP // 2):
            halves = []
            for e in range(2):
                hh = kh * GQA_GROUP + 2 * gp + e
                x = o[hh * rpp:(hh + 1) * rpp, (kh // 2) * LANES:(kh // 2 + 1) * LANES]
                halves.append(x if e == kh % 2 else pltpu.roll(x, HEAD_DIM, 1))
            dest = jnp.where(lo, halves[0], halves[1])
            c = kh * 2 + gp
            for tt in range(dec):
                o_s[pl.ds(pl.multiple_of(tt * nb + b0, SB), SB), c * LANES:(c + 1) * LANES] = dest[tt * SB:(tt + 1) * SB]

    for j in range(POOL_STATE):
        npool_ref[j] = seq[j + dec]

    @pl.when(i == nsteps - 1)
    def _():
        h = h_ref[...]
        xn = _rms_in(h, ng_ref[...])
        za = _dot(xn, win_ref[:, C_ZA:C_GA])
        yb = o_s[...] * (za * _sigmoid(za))
        y_ref[...] = _tail(h, p_ref[...], ya_s[...], yb, _dot(xn, win_ref[:, C_GA:C_GB]), _dot(xn, win_ref[:, C_GB:C_END]),
                           wpp_ref[...], wpa_ref[...], wout_ref[...], wg_ref[...], wple_ref[...])


def _sample_layer(h, p, st, ck, cv, tabs, bmain, bnew, sel, w, w_in_bf, layer, stacked):
    R = h.shape[0]
    nb = ck.shape[1]
    full = lambda shape: pl.BlockSpec(shape, lambda i: (0,) * len(shape), pipeline_mode=pl.Buffered(1))
    cache = pl.BlockSpec((None, SB, KV_WIDTH, WINDOW), lambda i: (layer, i, 0, 0))
    state = pl.BlockSpec((None, POOL_STATE, SB, D_MODEL), lambda i: (layer, 0, i, 0))
    in_specs = [
        pl.BlockSpec(memory_space=pltpu.SMEM),
        full((R, D_MODEL)), full((R, PLE_DIM)), state, cache, cache,
        full((R, LANES)), full((R, LANES)), full((R, LANES)), full(bmain.shape), full(bnew.shape), full(sel.shape),
    ] + _weight_specs(layer, full((D_MODEL, C_END)))
    out_specs = [full((R, D_MODEL)), cache, cache, state]
    out_shape = [jax.ShapeDtypeStruct((R, D_MODEL), F32), jax.ShapeDtypeStruct(ck.shape, F32),
                 jax.ShapeDtypeStruct(cv.shape, F32), jax.ShapeDtypeStruct(st.shape, F32)]
    scratch = [pltpu.VMEM((R, D_MODEL), F32) for _ in range(4)] + [pltpu.VMEM((R, KV_WIDTH), F32) for _ in range(2)]
    args = [w["sinks2"][layer], h, p, st, ck, cv, tabs[0], tabs[1], tabs[2], bmain, bnew, sel] + _weight_args(w, w_in_bf)
    aliases = {}
    if stacked is not None:
        aliases = {len(args) + k: 1 + k for k in range(len(stacked))}
        args += list(stacked)
        in_specs += [pl.BlockSpec(memory_space=pl.ANY)] * len(stacked)
    return pl.pallas_call(
        functools.partial(_sample_kernel, nb=nb, n_alias=len(aliases)),
        grid=(nb // SB,),
        in_specs=in_specs,
        out_specs=out_specs,
        out_shape=out_shape,
        scratch_shapes=scratch,
        input_output_aliases=aliases,
        compiler_params=pltpu.CompilerParams(dimension_semantics=("arbitrary",), vmem_limit_bytes=VMEM_LIMIT),
        name="sample_layer",
    )(*args)


def _rope_tables(pos):
    pos = np.asarray(pos, np.float64)
    inv_freq = ROPE_THETA ** (-np.arange(0, ROPE_DIM, 2, dtype=np.float64) / ROPE_DIM)
    ang = pos[:, None] * inv_freq[None, :]
    cos, sin = np.cos(ang), np.sin(ang)
    n = pos.shape[0]
    half = ROPE_DIM // 2
    c64 = np.concatenate([cos, cos, np.ones((n, HEAD_DIM - ROPE_DIM))], axis=1)
    a64 = np.concatenate([-sin, np.zeros((n, HEAD_DIM - half))], axis=1)
    b64 = np.concatenate([np.zeros((n, half)), sin, np.zeros((n, HEAD_DIM - ROPE_DIM))], axis=1)
    rep = LANES // HEAD_DIM
    return tuple(jnp.asarray(np.tile(x, (1, rep)), F32) for x in (c64, a64, b64))


def _prompt_bias():
    q = np.arange(WINDOW)[:, None]
    j = np.arange(WINDOW)[None, :]
    prev = np.where(j >= q, 0.0, NEG)
    cur = np.where(j <= q, 0.0, NEG)
    normal = np.concatenate([prev, cur], axis=1)
    first = np.concatenate([np.full_like(prev, NEG), cur], axis=1)
    return jnp.asarray(np.stack([normal, first]), F32)


def _sample_bias(dec):
    rq = np.arange(dec * SB)
    tq, bq = rq // SB, rq % SB
    cm = np.arange(SB * WINDOW)
    bm, jm = cm // WINDOW, cm % WINDOW
    main = np.where((bq[:, None] == bm[None, :]) & (jm[None, :] >= tq[:, None]), 0.0, NEG)
    new = np.where((bq[:, None] == bq[None, :]) & (tq[None, :] <= tq[:, None]), 0.0, NEG)
    sel = np.zeros((SB, dec * SB, WINDOW))
    for r in rq:
        sel[bq[r], r, WINDOW - dec + tq[r]] = 1.0
    return jnp.asarray(main, F32), jnp.asarray(new, F32), jnp.asarray(sel, BF16)


def _prep_weights(norm_g, w_in, q_norm_g, k_norm_g, sinks, pool_map, pool_scale,
                  w_proj_pool, w_proj_attn, w_out, w_ple, w_ple_gate):
    qscale = HEAD_DIM ** -0.5 * LOG2E
    blk = np.kron(np.eye(256 // HEAD_DIM), np.full((HEAD_DIM, HEAD_DIM), 1.0 / HEAD_DIM))
    return dict(
        sinks2=sinks * LOG2E,
        norm_g=norm_g[:, None, :],
        gq=jnp.tile(q_norm_g * qscale, (1, N_HEADS))[:, None, :],
        gk=jnp.tile(k_norm_g, (1, N_KV_HEADS))[:, None, :],
        pool_scale=pool_scale[:, None, :],
        w_in_f32=w_in,
        hn=jnp.asarray(blk, BF16),
        pool_map=pool_map.astype(BF16),
        w_pp=w_proj_pool.astype(BF16), w_pa=w_proj_attn.astype(BF16),
        w_out=w_out.astype(BF16), w_g=w_ple_gate.astype(BF16), w_ple=w_ple.astype(BF16),
    )


def kernel(x_prompt, x_sample, cache_k, cache_v, state_pool, p_prompt, p_sample, norm_g, w_in, q_norm_g, k_norm_g, sinks, pool_map, pool_scale, w_proj_pool, w_proj_attn, w_out, w_ple, w_ple_gate):
    depth = w_in.shape[0]
    B, T, _ = x_prompt.shape
    nb, dec, _ = x_sample.shape
    assert T % TQ == 0 and TQ % WINDOW == 0 and nb % SB == 0
    ptabs = _rope_tables(np.arange(T))
    stabs = _rope_tables(np.repeat(PAST_LEN + np.arange(dec), nb))
    pbias = _prompt_bias()
    bmain, bnew, sel = _sample_bias(dec)

    hp = x_prompt
    hs = x_sample.transpose(1, 0, 2).reshape(dec * nb, D_MODEL)
    ck_all = cache_k.transpose(0, 1, 3, 4, 2).reshape(depth, nb, KV_WIDTH, WINDOW)
    cv_all = cache_v.transpose(0, 1, 3, 4, 2).reshape(depth, nb, KV_WIDTH, WINDOW)
    st_all = state_pool.transpose(0, 2, 1, 3)
    stacked = None
    kp_l, vp_l, pp_l = [], [], []
    w = _prep_weights(norm_g, w_in, q_norm_g, k_norm_g, sinks, pool_map, pool_scale,
                      w_proj_pool, w_proj_attn, w_out, w_ple, w_ple_gate)
    for i in range(depth):
        hp, nk, nv, npool, w_in_bf = _prompt_layer(hp, p_prompt, ptabs, pbias, w, i)
        kp_l.append(nk.reshape(B, WINDOW, N_KV_HEADS, HEAD_DIM))
        vp_l.append(nv.reshape(B, WINDOW, N_KV_HEADS, HEAD_DIM))
        pp_l.append(npool)

        ps_t = p_sample[i].transpose(1, 0, 2).reshape(dec * nb, PLE_DIM)
        hs, *stacked = _sample_layer(hs, ps_t, st_all, ck_all, cv_all, stabs, bmain, bnew, sel, w, w_in_bf, i, stacked)
    uncache = lambda a: a.reshape(depth, nb, N_KV_HEADS, HEAD_DIM, WINDOW).transpose(0, 1, 4, 2, 3)
    new_k_sample, new_v_sample = uncache(stacked[0]), uncache(stacked[1])
    new_pool_sample = stacked[2].transpose(0, 2, 1, 3)
    y_sample = hs.reshape(dec, nb, D_MODEL).transpose(1, 0, 2)
    return (hp, y_sample, jnp.stack(kp_l), jnp.stack(vp_l), jnp.stack(pp_l),
            new_k_sample, new_v_sample, new_pool_sample)
```

```python
import functools
import math

import numpy as np
import jax
import jax.numpy as jnp
from jax import lax
from jax.experimental import pallas as pl
from jax.experimental.pallas import tpu as pltpu

D_MODEL = 1024
PLE_DIM = 256
POOL_WINDOWS = (2, 4, 8, 16)
POOL_GROUP = D_MODEL // len(POOL_WINDOWS)
POOL_STATE = max(POOL_WINDOWS) - 1
HEAD_DIM = 64
N_HEADS = D_MODEL // HEAD_DIM
N_KV_HEADS = 4
GQA_GROUP = N_HEADS // N_KV_HEADS
KV_WIDTH = N_KV_HEADS * HEAD_DIM
WINDOW = 128
ROPE_THETA = 500000.0
ROPE_DIM = HEAD_DIM // 4
EPS = 1e-6
PAST_LEN = 16384

C_U, C_ZP, C_Q, C_K, C_V, C_ZA, C_GA, C_GB, C_END = 0, 1024, 2048, 3072, 3328, 3584, 4608, 5632, 6656

LANES = 128
SUBLANES = 8
NEG = -1e30
LOG2E = math.log2(math.e)
VMEM_LIMIT = 60 * 1024 * 1024

TQ = 512
FCH = 256
WCH = 32
WSLOTS = 4
SB = 8

F32 = jnp.float32
BF16 = jnp.bfloat16


def _sigmoid(x):
    return 1.0 / (1.0 + jnp.exp2(x * (-LOG2E)))


def _dot(a, b):
    return jnp.dot(a, b, preferred_element_type=F32)


def _dot_t(a, b):
    return lax.dot_general(a, b, (((1,), (1,)), ((), ())), preferred_element_type=F32)


def _rope(x, cos, s1, s2):
    outs = []
    for c in range(x.shape[1] // LANES):
        xc = x[:, c * LANES:(c + 1) * LANES]
        outs.append(xc * cos + pltpu.roll(xc, LANES - ROPE_DIM // 2, 1) * s1 + pltpu.roll(xc, ROPE_DIM // 2, 1) * s2)
    return outs[0] if len(outs) == 1 else jnp.concatenate(outs, axis=1)


def _head_norm(x, hn, gain):
    sq = (x * x).astype(BF16)
    ms = jnp.concatenate([_dot(sq[:, c * 256:(c + 1) * 256], hn) for c in range(x.shape[1] // 256)], axis=1)
    return x * lax.rsqrt(ms + EPS) * gain


def _rms_in(h, g):
    ms = jnp.mean(h * h, axis=-1, keepdims=True)
    return (h * lax.rsqrt(ms + EPS) * g).astype(BF16)


def _tail(h, p, ya, yb, ga, gb, wpp, wpa, wout, wg, wple):
    m = _sigmoid(ga) * _dot(ya.astype(BF16), wpp) + _sigmoid(gb) * _dot(yb.astype(BF16), wpa)
    h1 = h + _dot(m.astype(BF16), wout)
    gate = _sigmoid(_dot(h1.astype(BF16), wg))
    return h1 + gate * _dot(p.astype(BF16), wple)


def _pool_map(r, pm_ref, ps):
    rb = r.astype(BF16)
    mapped = jnp.concatenate([_dot(rb[:, g * POOL_GROUP:(g + 1) * POOL_GROUP], pm_ref[g])
                              for g in range(len(POOL_WINDOWS))], axis=1)
    return mapped * ps


def _prompt_kernel(sinks_ref, h_ref, p_ref, tab_ref, bias_ref,
                   ng_ref, gq_ref, gk_ref, ps_ref, winf_hbm, hn_ref, pm_ref,
                   wpp_ref, wpa_ref, wout_ref, wg_ref, wple_ref,
                   y_ref, nk_ref, nv_ref, npool_ref, winb_hbm,
                   kall, vall, u_s, s2_s, s4_s, s8_s, o_s, win_ref, stg, sem, *, layer):
    t = pl.program_id(1)
    first_step = jnp.logical_and(pl.program_id(0) == 0, t == 0)
    last_step = jnp.logical_and(pl.program_id(0) == pl.num_programs(0) - 1, t == pl.num_programs(1) - 1)
    nqb = TQ // WINDOW

    def stage(c):
        slot = c % WSLOTS
        return pltpu.make_async_copy(winf_hbm.at[layer, pl.ds(c * WCH, WCH), :], stg.at[slot], sem.at[slot])

    publish = pltpu.make_async_copy(win_ref, winb_hbm, sem.at[WSLOTS])

    @pl.when(first_step)
    def _():
        nch = D_MODEL // WCH
        for c in range(WSLOTS - 1):
            stage(c).start()
        for c in range(nch):
            if c + WSLOTS - 1 < nch:
                stage(c + WSLOTS - 1).start()
            stage(c).wait()
            win_ref[c * WCH:(c + 1) * WCH, :] = stg[c % WSLOTS].astype(BF16)
        publish.start()
    lane = lax.broadcasted_iota(jnp.int32, (WINDOW, LANES), 1)
    lo = lane < HEAD_DIM

    @pl.when(t == 0)
    def _():
        kall[...] = jnp.zeros_like(kall)
        vall[:, :, 0:LANES] = jnp.zeros((N_KV_HEADS, (nqb + 1) * 256, LANES), BF16)
        ones_pat = jnp.concatenate([jnp.where(lo, 1.0, 0.0), jnp.where(lo, 0.0, 1.0)], axis=0).astype(BF16)
        for kh in range(N_KV_HEADS):
            for x in range(nqb + 1):
                vall[kh, x * 256:(x + 1) * 256, LANES:2 * LANES] = ones_pat
        u_s[0:16, :] = jnp.zeros((16, D_MODEL), F32)
        s2_s[0:16, :] = jnp.zeros((16, D_MODEL), F32)
        s4_s[0:16, :] = jnp.zeros((16, 768), F32)
        s8_s[0:16, :] = jnp.zeros((16, 512), F32)

    @pl.when(t > 0)
    def _():
        for kh in range(N_KV_HEADS):
            kall[kh, 0:256, :] = kall[kh, nqb * 256:(nqb + 1) * 256, :]
            vall[kh, 0:256, 0:LANES] = vall[kh, nqb * 256:(nqb + 1) * 256, 0:LANES]

    h = h_ref[0]
    xn = _rms_in(h, ng_ref[...])

    def proj(c0, c1):
        return _dot(xn, win_ref[:, c0:c1])

    res = {}

    def fillers():
        parts = []
        for c in range(D_MODEL // FCH):
            parts.append(proj(C_U + c * FCH, C_U + (c + 1) * FCH))
            yield
        u = jnp.concatenate(parts, axis=1)
        u_s[16:16 + TQ, :] = u
        s2 = u + u_s[15:15 + TQ, :]
        s2_s[16:16 + TQ, :] = s2
        s4 = s2[:, 256:] + s2_s[14:14 + TQ, 256:]
        s4_s[16:16 + TQ, :] = s4
        s8 = s4[:, 256:] + s4_s[12:12 + TQ, 256:]
        s8_s[16:16 + TQ, :] = s8
        s16 = s8[:, 256:] + s8_s[8:8 + TQ, 256:]
        pos1 = (t * TQ + 1 + lax.broadcasted_iota(jnp.int32, (TQ, 1), 0)).astype(F32)
        wins = (s2[:, :256], s4[:, :256], s8[:, :256], s16)
        r = jnp.concatenate([wins[g] * (1.0 / jnp.minimum(pos1, float(w))) for g, w in enumerate(POOL_WINDOWS)],
                            axis=1) - u
        npool_ref[0] = u_s[TQ + 1:TQ + 16, :]
        u_s[0:16, :] = u_s[TQ:TQ + 16, :]
        s2_s[0:16, :] = s2_s[TQ:TQ + 16, :]
        s4_s[0:16, :] = s4_s[TQ:TQ + 16, :]
        s8_s[0:16, :] = s8_s[TQ:TQ + 16, :]
        mapped = _pool_map(r, pm_ref, ps_ref[...])
        yield
        for name, c_lo in (("zp", C_ZP), ("za", C_ZA), ("ga", C_GA), ("gb", C_GB)):
            parts = []
            for c in range(D_MODEL // FCH):
                parts.append(proj(c_lo + c * FCH, c_lo + (c + 1) * FCH))
                yield
            res[name] = jnp.concatenate(parts, axis=1)
        yab = (mapped * (res["zp"] * _sigmoid(res["zp"]))).astype(BF16)
        pb = p_ref[0].astype(BF16)
        for name, lhs, w_ref in (("pp", yab, wpp_ref), ("ple", pb, wple_ref)):
            parts = []
            for c in range(D_MODEL // FCH):
                parts.append(_dot(lhs, w_ref[:, c * FCH:(c + 1) * FCH]))
                yield
            res[name] = jnp.concatenate(parts, axis=1)

    def tiled_bias(b):
        b4 = jnp.concatenate([b[:, :LANES], b[:, :LANES], b[:, LANES:], b[:, LANES:]], axis=1)
        return jnp.concatenate([b4, b4], axis=0)

    biases = [tiled_bias(jnp.where(t == 0, bias_ref[1], bias_ref[0]))] + [tiled_bias(bias_ref[0])] * (nqb - 1)
    chains = [(kh, n) for kh in range(N_KV_HEADS) for n in range(nqb)]
    n_fill = 7 * (D_MODEL // FCH) + 1
    fill = fillers()

    def scores(kh, n):
        rows = slice(n * WINDOW, (n + 1) * WINDOW)
        qs = jnp.concatenate([qb[rows, kh * 256:kh * 256 + LANES],
                              qb[rows, kh * 256 + LANES:(kh + 1) * 256]], axis=0)
        return _dot_t(qs, kall[kh, n * 256:n * 256 + 512, :]) + biases[n]

    q_raw, k_raw, v = proj(C_Q, C_K), proj(C_K, C_V), proj(C_V, C_ZA)
    pre = D_MODEL // FCH + 1
    for _ in range(pre):
        next(fill)
    done = pre
    cos, s1, s2t = (tab_ref[:, k * LANES:(k + 1) * LANES] for k in range(3))
    hn = hn_ref[...]
    kr = _rope(_head_norm(k_raw, hn, gk_ref[...]), cos, s1, s2t)
    qb = _rope(_head_norm(q_raw, hn, gq_ref[...]), cos, s1, s2t).astype(BF16)
    nk_ref[0] = kr[TQ - WINDOW:, :]
    nv_ref[0] = v[TQ - WINDOW:, :]
    for x in range(nqb):
        rows = slice(x * WINDOW, (x + 1) * WINDOW)
        base = (x + 1) * 256
        for pr in range(N_KV_HEADS // 2):
            for src, dst, col in ((kr, kall, None), (v, vall, slice(0, LANES))):
                a = src[rows, pr * LANES:(pr + 1) * LANES]
                ra = pltpu.roll(a, HEAD_DIM, 1)
                parts = ((jnp.where(lo, a, 0.0), jnp.where(lo, 0.0, ra)),
                         (jnp.where(lo, ra, 0.0), jnp.where(lo, 0.0, a)))
                for e in range(2):
                    kh = 2 * pr + e
                    lh = jnp.concatenate(parts[e], axis=0).astype(BF16)
                    if col is None:
                        dst[kh, base:base + 256, :] = lh
                    else:
                        dst[kh, base:base + 256, col] = lh

    s_next = scores(*chains[0])
    for ci, (kh, n) in enumerate(chains):
        s = s_next
        if ci + 1 < len(chains):
            s_next = scores(*chains[ci + 1])
        want = pre + ((n_fill - pre) * (ci + 1)) // len(chains)
        while done < want:
            next(fill)
            done += 1
        rows = slice(n * WINDOW, (n + 1) * WINDOW)
        p_rows, sink_rows = [], []
        for rr in range(2):
            rs = slice(rr * WINDOW, (rr + 1) * WINDOW)
            pcols = [None] * 4
            sterm = []
            for e in range(2):
                sink = sinks_ref[kh * GQA_GROUP + 2 * rr + e]
                sp = s[rs, e * LANES:(e + 1) * LANES]
                sc = s[rs, 256 + e * LANES:256 + (e + 1) * LANES]
                mx = jnp.maximum(jnp.max(jnp.maximum(sp, sc), axis=-1, keepdims=True), sink)
                pcols[e] = jnp.exp2(sp - mx)
                pcols[2 + e] = jnp.exp2(sc - mx)
                sterm.append(jnp.exp2(sink - mx))
            p_rows.append(jnp.concatenate(pcols, axis=1))
            sink_rows.append(jnp.where(lo, sterm[0], sterm[1]))
        pmat = jnp.concatenate(p_rows, axis=0).astype(BF16)
        o2 = _dot(pmat, vall[kh, n * 256:n * 256 + 512, :])
        o = o2[:, :LANES] / (o2[:, LANES:] + jnp.concatenate(sink_rows, axis=0))
        o_s[rows, kh * 256:kh * 256 + LANES] = o[:WINDOW]
        o_s[rows, kh * 256 + LANES:(kh + 1) * 256] = o[WINDOW:]
    for _ in fill:
        pass

    za, ga, gb = res["za"], res["ga"], res["gb"]
    yb = o_s[...] * (za * _sigmoid(za))
    m = _sigmoid(ga) * res["pp"] + _sigmoid(gb) * _dot(yb.astype(BF16), wpa_ref[...])
    h1 = h + _dot(m.astype(BF16), wout_ref[...])
    gate = _sigmoid(_dot(h1.astype(BF16), wg_ref[...]))
    y_ref[0] = h1 + gate * res["ple"]

    @pl.when(last_step)
    def _():
        publish.wait()


def _const_spec(shape):
    nd = len(shape)
    return pl.BlockSpec(shape, lambda *_: (0,) * nd, pipeline_mode=pl.Buffered(1))


def _layer_spec(shape, layer):
    nd = len(shape)
    return pl.BlockSpec((None,) + tuple(shape), lambda *_: (layer,) + (0,) * nd, pipeline_mode=pl.Buffered(1))


def _weight_specs(layer, win_spec):
    ls = lambda *shape: _layer_spec(shape, layer)
    return [ls(1, D_MODEL), ls(1, D_MODEL), ls(1, KV_WIDTH), ls(1, D_MODEL),
            win_spec, _const_spec((256, 256)), ls(4, POOL_GROUP, POOL_GROUP),
            ls(D_MODEL, D_MODEL), ls(D_MODEL, D_MODEL), ls(D_MODEL, D_MODEL), ls(D_MODEL, D_MODEL), ls(PLE_DIM, D_MODEL)]


def _weight_args(w, w_in):
    return [w["norm_g"], w["gq"], w["gk"], w["pool_scale"], w_in, w["hn"], w["pool_map"],
            w["w_pp"], w["w_pa"], w["w_out"], w["w_g"], w["w_ple"]]


def _prompt_layer(h, p, tabs, bias, w, layer):
    B, T, _ = h.shape
    nt = T // TQ
    nqb = TQ // WINDOW
    row = lambda width: pl.BlockSpec((1, TQ, width), lambda b, t: (b, t, 0))
    tab = pl.BlockSpec((TQ, 3 * LANES), lambda b, t: (t, 0))
    in_specs = [
        pl.BlockSpec(memory_space=pltpu.SMEM),
        row(D_MODEL), pl.BlockSpec((None, 1, TQ, PLE_DIM), lambda b, t: (layer, b, t, 0)), tab,
        _const_spec(bias.shape),
    ] + _weight_specs(layer, pl.BlockSpec(memory_space=pl.ANY))
    out_specs = [
        row(D_MODEL),
        pl.BlockSpec((1, WINDOW, KV_WIDTH), lambda b, t: (b, 0, 0)),
        pl.BlockSpec((1, WINDOW, KV_WIDTH), lambda b, t: (b, 0, 0)),
        pl.BlockSpec((1, POOL_STATE, D_MODEL), lambda b, t: (b, 0, 0)),
        pl.BlockSpec(memory_space=pl.ANY),
    ]
    out_shape = [
        jax.ShapeDtypeStruct((B, T, D_MODEL), F32),
        jax.ShapeDtypeStruct((B, WINDOW, KV_WIDTH), F32),
        jax.ShapeDtypeStruct((B, WINDOW, KV_WIDTH), F32),
        jax.ShapeDtypeStruct((B, POOL_STATE, D_MODEL), F32),
        jax.ShapeDtypeStruct((D_MODEL, C_END), BF16),
    ]
    scratch = [
        pltpu.VMEM((N_KV_HEADS, (nqb + 1) * 256, LANES), BF16),
        pltpu.VMEM((N_KV_HEADS, (nqb + 1) * 256, 2 * LANES), BF16),
        pltpu.VMEM((TQ + 16, D_MODEL), F32),
        pltpu.VMEM((TQ + 16, D_MODEL), F32),
        pltpu.VMEM((TQ + 16, 768), F32),
        pltpu.VMEM((TQ + 16, 512), F32),
        pltpu.VMEM((TQ, D_MODEL), F32),
        pltpu.VMEM((D_MODEL, C_END), BF16),
        pltpu.VMEM((WSLOTS, WCH, C_END), F32),
        pltpu.SemaphoreType.DMA((WSLOTS + 1,)),
    ]
    return pl.pallas_call(
        functools.partial(_prompt_kernel, layer=layer),
        grid=(B, nt),
        in_specs=in_specs,
        out_specs=out_specs,
        out_shape=out_shape,
        scratch_shapes=scratch,
        compiler_params=pltpu.CompilerParams(dimension_semantics=("arbitrary", "arbitrary"),
                                             vmem_limit_bytes=VMEM_LIMIT),
        name="prompt_layer",
    )(w["sinks2"][layer], h, p, tabs, bias, *_weight_args(w, w["w_in_f32"]))


def _split3(x):
    hi = x.astype(BF16)
    r1 = x - hi.astype(F32)
    mid = r1.astype(BF16)
    return hi, mid, (r1 - mid.astype(F32)).astype(BF16)


def _sample_kernel(sinks_ref, h_ref, p_ref, st_ref, ck_ref, cv_ref, cos_ref, s1_ref, s2_ref, bmain_ref, bnew_ref, sel_ref,
                   ng_ref, gq_ref, gk_ref, ps_ref, win_ref, hn_ref, pm_ref,
                   wpp_ref, wpa_ref, wout_ref, wg_ref, wple_ref, *rest, nb, n_alias):
    y_ref, nkc_ref, nvc_ref, npool_ref, q_s, o_s, ya_s, nu_ref, nk_ref, nv_ref = rest[n_alias:]
    i = pl.program_id(0)
    nsteps = pl.num_programs(0)
    dec = h_ref.shape[0] // nb

    @pl.when(i == 0)
    def _():
        h = h_ref[...]
        xn = _rms_in(h, ng_ref[...])

        def proj(c0, c1):
            return _dot(xn, win_ref[:, c0:c1])

        nu_ref[...] = proj(C_U, C_ZP)
        zp = proj(C_ZP, C_Q)
        ya_s[...] = zp * _sigmoid(zp)

        cos, s1, s2t = cos_ref[...], s1_ref[...], s2_ref[...]
        hn = hn_ref[...]
        q_s[...] = _rope(_head_norm(proj(C_Q, C_K), hn, gq_ref[...]), cos, s1, s2t)
        nk_ref[...] = _rope(_head_norm(proj(C_K, C_V), hn, gk_ref[...]), cos, s1, s2t)
        nv_ref[...] = proj(C_V, C_ZA)

    b0 = pl.multiple_of(i * SB, SB)
    rpp = dec * SB
    nrow = N_HEADS * rpp
    lo = lax.broadcasted_iota(jnp.int32, (rpp, LANES), 1) < HEAD_DIM
    zero = jnp.zeros((rpp, LANES), F32)

    def rows_tb(ref, c0, c1):
        return jnp.concatenate([ref[pl.ds(pl.multiple_of(tt * nb + b0, SB), SB), c0:c1] for tt in range(dec)], axis=0)

    u_tb = rows_tb(nu_ref, 0, D_MODEL)
    seq = [st_ref[j] for j in range(POOL_STATE)] + [u_tb[tt * SB:(tt + 1) * SB] for tt in range(dec)]
    r_rows = []
    for tt in range(dec):
        e = POOL_STATE + tt
        cols = []
        for g, w in enumerate(POOL_WINDOWS):
            cs = slice(g * POOL_GROUP, (g + 1) * POOL_GROUP)
            acc = seq[e][:, cs]
            for d in range(1, w):
                acc = acc + seq[e - d][:, cs]
            cols.append(acc * (1.0 / w))
        r_rows.append(jnp.concatenate(cols, axis=1) - seq[e])
    ya_tb = _pool_map(jnp.concatenate(r_rows, axis=0), pm_ref, ps_ref[...])
    for tt in range(dec):
        rs = pl.ds(pl.multiple_of(tt * nb + b0, SB), SB)
        ya_s[rs, :] = ya_s[rs, :] * ya_tb[tt * SB:(tt + 1) * SB]

    slot = lax.broadcasted_iota(jnp.int32, (KV_WIDTH, WINDOW), 1)
    knew_f = rows_tb(nk_ref, 0, KV_WIDTH)
    vnew_f = rows_tb(nv_ref, 0, KV_WIDTH)

    def new_cache(src_ref, new_f, dst_ref):
        pieces3 = _split3(new_f)
        for bl in range(SB):
            cols = None
            for pc in pieces3:
                part = lax.dot_general(pc, sel_ref[bl], (((0,), (0,)), ((), ())), preferred_element_type=F32)
                cols = part if cols is None else cols + part
            dst_ref[bl] = jnp.where(slot < WINDOW - dec, pltpu.roll(src_ref[bl], WINDOW - dec, 1), cols)

    pieces = []
    for kh in range(N_KV_HEADS):
        for gp in range(GQA_GROUP // 2):
            c = kh * 2 + gp
            x = rows_tb(q_s, c * LANES, (c + 1) * LANES)
            rx = pltpu.roll(x, HEAD_DIM, 1)
            for e in range(2):
                src = x if e == kh % 2 else rx
                half = jnp.where(lo, src, 0.0) if kh % 2 == 0 else jnp.where(lo, 0.0, src)
                pieces.append(jnp.concatenate([half, zero] if kh < 2 else [zero, half], axis=1))
    lhs = jnp.concatenate(pieces, axis=0).astype(BF16)
    kmain = jnp.concatenate([ck_ref[bl].astype(BF16) for bl in range(SB)], axis=1)
    vmain = jnp.concatenate([cv_ref[bl].astype(BF16) for bl in range(SB)], axis=1)
    knew = knew_f.astype(BF16)
    vnew = vnew_f.astype(BF16)
    s_main = (_dot(lhs, kmain).reshape(N_HEADS, rpp, SB * WINDOW) + bmain_ref[...][None]).reshape(nrow, SB * WINDOW)
    s_new = (_dot_t(lhs, knew).reshape(N_HEADS, rpp, rpp) + bnew_ref[...][None]).reshape(nrow, rpp)

    new_cache(ck_ref, knew_f, nkc_ref)
    new_cache(cv_ref, vnew_f, nvc_ref)

    sink = jnp.concatenate([jnp.full((rpp, 1), sinks_ref[hh], F32) for hh in range(N_HEADS)], axis=0)
    mx = jnp.maximum(jnp.maximum(jnp.max(s_main, axis=-1, keepdims=True), jnp.max(s_new, axis=-1, keepdims=True)), sink)
    p_main = jnp.exp2(s_main - mx).astype(BF16)
    p_new = jnp.exp2(s_new - mx).astype(BF16)
    den = (jnp.sum(p_main.astype(F32), axis=-1, keepdims=True) + jnp.sum(p_new.astype(F32), axis=-1, keepdims=True)
           + jnp.exp2(sink - mx))
    o = (_dot_t(p_main, vmain) + _dot(p_new, vnew)) / den
    for kh in range(N_KV_HEADS):
        for gp in range(GQA_GROUP // 2):
            halves = []
            for e in range(2):
                hh = kh * GQA_GROUP + 2 * gp + e
                x = o[hh * rpp:(hh + 1) * rpp, (kh // 2) * LANES:(kh // 2 + 1) * LANES]
                halves.append(x if e == kh % 2 else pltpu.roll(x, HEAD_DIM, 1))
            dest = jnp.where(lo, halves[0], halves[1])
            c = kh * 2 + gp
            for tt in range(dec):
                o_s[pl.ds(pl.multiple_of(tt * nb + b0, SB), SB), c * LANES:(c + 1) * LANES] = dest[tt * SB:(tt + 1) * SB]

    for j in range(POOL_STATE):
        npool_ref[j] = seq[j + dec]

    @pl.when(i == nsteps - 1)
    def _():
        h = h_ref[...]
        xn = _rms_in(h, ng_ref[...])
        za = _dot(xn, win_ref[:, C_ZA:C_GA])
        yb = o_s[...] * (za * _sigmoid(za))
        y_ref[...] = _tail(h, p_ref[...], ya_s[...], yb, _dot(xn, win_ref[:, C_GA:C_GB]), _dot(xn, win_ref[:, C_GB:C_END]),
                           wpp_ref[...], wpa_ref[...], wout_ref[...], wg_ref[...], wple_ref[...])


def _sample_layer(h, p, st, ck, cv, tabs, bmain, bnew, sel, w, w_in_bf, layer, stacked):
    R = h.shape[0]
    nb = ck.shape[1]
    full = lambda shape: pl.BlockSpec(shape, lambda i: (0,) * len(shape), pipeline_mode=pl.Buffered(1))
    cache = pl.BlockSpec((None, SB, KV_WIDTH, WINDOW), lambda i: (layer, i, 0, 0))
    state = pl.BlockSpec((None, POOL_STATE, SB, D_MODEL), lambda i: (layer, 0, i, 0))
    in_specs = [
        pl.BlockSpec(memory_space=pltpu.SMEM),
        full((R, D_MODEL)), full((R, PLE_DIM)), state, cache, cache,
        full((R, LANES)), full((R, LANES)), full((R, LANES)), full(bmain.shape), full(bnew.shape), full(sel.shape),
    ] + _weight_specs(layer, full((D_MODEL, C_END)))
    out_specs = [full((R, D_MODEL)), cache, cache, state]
    out_shape = [jax.ShapeDtypeStruct((R, D_MODEL), F32), jax.ShapeDtypeStruct(ck.shape, F32),
                 jax.ShapeDtypeStruct(cv.shape, F32), jax.ShapeDtypeStruct(st.shape, F32)]
    scratch = [pltpu.VMEM((R, D_MODEL), F32) for _ in range(4)] + [pltpu.VMEM((R, KV_WIDTH), F32) for _ in range(2)]
    args = [w["sinks2"][layer], h, p, st, ck, cv, tabs[0], tabs[1], tabs[2], bmain, bnew, sel] + _weight_args(w, w_in_bf)
    aliases = {}
    if stacked is not None:
        aliases = {len(args) + k: 1 + k for k in range(len(stacked))}
        args += list(stacked)
        in_specs += [pl.BlockSpec(memory_space=pl.ANY)] * len(stacked)
    return pl.pallas_call(
        functools.partial(_sample_kernel, nb=nb, n_alias=len(aliases)),
        grid=(nb // SB,),
        in_specs=in_specs,
        out_specs=out_specs,
        out_shape=out_shape,
        scratch_shapes=scratch,
        input_output_aliases=aliases,
        compiler_params=pltpu.CompilerParams(dimension_semantics=("arbitrary",), vmem_limit_bytes=VMEM_LIMIT),
        name="sample_layer",
    )(*args)


def _rope_tables(pos, merged=False):
    pos = np.asarray(pos, np.float64)
    inv_freq = ROPE_THETA ** (-np.arange(0, ROPE_DIM, 2, dtype=np.float64) / ROPE_DIM)
    ang = pos[:, None] * inv_freq[None, :]
    cos, sin = np.cos(ang), np.sin(ang)
    n = pos.shape[0]
    half = ROPE_DIM // 2
    c64 = np.concatenate([cos, cos, np.ones((n, HEAD_DIM - ROPE_DIM))], axis=1)
    a64 = np.concatenate([-sin, np.zeros((n, HEAD_DIM - half))], axis=1)
    b64 = np.concatenate([np.zeros((n, half)), sin, np.zeros((n, HEAD_DIM - ROPE_DIM))], axis=1)
    rep = LANES // HEAD_DIM
    tabs = [np.tile(x, (1, rep)) for x in (c64, a64, b64)]
    if merged:
        return jnp.asarray(np.concatenate(tabs, axis=1), F32)
    return tuple(jnp.asarray(x, F32) for x in tabs)


def _prompt_bias():
    q = np.arange(WINDOW)[:, None]
    j = np.arange(WINDOW)[None, :]
    prev = np.where(j >= q, 0.0, NEG)
    cur = np.where(j <= q, 0.0, NEG)
    normal = np.concatenate([prev, cur], axis=1)
    first = np.concatenate([np.full_like(prev, NEG), cur], axis=1)
    return jnp.asarray(np.stack([normal, first]), F32)


def _sample_bias(dec):
    rq = np.arange(dec * SB)
    tq, bq = rq // SB, rq % SB
    cm = np.arange(SB * WINDOW)
    bm, jm = cm // WINDOW, cm % WINDOW
    main = np.where((bq[:, None] == bm[None, :]) & (jm[None, :] >= tq[:, None]), 0.0, NEG)
    new = np.where((bq[:, None] == bq[None, :]) & (tq[None, :] <= tq[:, None]), 0.0, NEG)
    sel = np.zeros((SB, dec * SB, WINDOW))
    for r in rq:
        sel[bq[r], r, WINDOW - dec + tq[r]] = 1.0
    return jnp.asarray(main, F32), jnp.asarray(new, F32), jnp.asarray(sel, BF16)


def _prep_weights(norm_g, w_in, q_norm_g, k_norm_g, sinks, pool_map, pool_scale,
                  w_proj_pool, w_proj_attn, w_out, w_ple, w_ple_gate):
    qscale = HEAD_DIM ** -0.5 * LOG2E
    blk = np.kron(np.eye(256 // HEAD_DIM), np.full((HEAD_DIM, HEAD_DIM), 1.0 / HEAD_DIM))
    return dict(
        sinks2=sinks * LOG2E,
        norm_g=norm_g[:, None, :],
        gq=jnp.tile(q_norm_g * qscale, (1, N_HEADS))[:, None, :],
        gk=jnp.tile(k_norm_g, (1, N_KV_HEADS))[:, None, :],
        pool_scale=pool_scale[:, None, :],
        w_in_f32=w_in,
        hn=jnp.asarray(blk, BF16),
        pool_map=pool_map.astype(BF16),
        w_pp=w_proj_pool.astype(BF16), w_pa=w_proj_attn.astype(BF16),
        w_out=w_out.astype(BF16), w_g=w_ple_gate.astype(BF16), w_ple=w_ple.astype(BF16),
    )


def kernel(x_prompt, x_sample, cache_k, cache_v, state_pool, p_prompt, p_sample, norm_g, w_in, q_norm_g, k_norm_g, sinks, pool_map, pool_scale, w_proj_pool, w_proj_attn, w_out, w_ple, w_ple_gate):
    depth = w_in.shape[0]
    B, T, _ = x_prompt.shape
    nb, dec, _ = x_sample.shape
    assert T % TQ == 0 and TQ % WINDOW == 0 and nb % SB == 0
    ptabs = _rope_tables(np.arange(T), merged=True)
    stabs = _rope_tables(np.repeat(PAST_LEN + np.arange(dec), nb))
    pbias = _prompt_bias()
    bmain, bnew, sel = _sample_bias(dec)

    hp = x_prompt
    hs = x_sample.transpose(1, 0, 2).reshape(dec * nb, D_MODEL)
    ck_all = cache_k.transpose(0, 1, 3, 4, 2).reshape(depth, nb, KV_WIDTH, WINDOW)
    cv_all = cache_v.transpose(0, 1, 3, 4, 2).reshape(depth, nb, KV_WIDTH, WINDOW)
    st_all = state_pool.transpose(0, 2, 1, 3)
    stacked = None
    kp_l, vp_l, pp_l = [], [], []
    w = _prep_weights(norm_g, w_in, q_norm_g, k_norm_g, sinks, pool_map, pool_scale,
                      w_proj_pool, w_proj_attn, w_out, w_ple, w_ple_gate)
    for i in range(depth):
        hp, nk, nv, npool, w_in_bf = _prompt_layer(hp, p_prompt, ptabs, pbias, w, i)
        kp_l.append(nk.reshape(B, WINDOW, N_KV_HEADS, HEAD_DIM))
        vp_l.append(nv.reshape(B, WINDOW, N_KV_HEADS, HEAD_DIM))
        pp_l.append(npool)

        ps_t = p_sample[i].transpose(1, 0, 2).reshape(dec * nb, PLE_DIM)
        hs, *stacked = _sample_layer(hs, ps_t, st_all, ck_all, cv_all, stabs, bmain, bnew, sel, w, w_in_bf, i, stacked)
    uncache = lambda a: a.reshape(depth, nb, N_KV_HEADS, HEAD_DIM, WINDOW).transpose(0, 1, 4, 2, 3)
    new_k_sample, new_v_sample = uncache(stacked[0]), uncache(stacked[1])
    new_pool_sample = stacked[2].transpose(0, 2, 1, 3)
    y_sample = hs.reshape(dec, nb, D_MODEL).transpose(1, 0, 2)
    return (hp, y_sample, jnp.stack(kp_l), jnp.stack(vp_l), jnp.stack(pp_l),
            new_k_sample, new_v_sample, new_pool_sample)
```

```python
import functools
import math

import numpy as np
import jax
import jax.numpy as jnp
from jax import lax
from jax.experimental import pallas as pl
from jax.experimental.pallas import tpu as pltpu

D_MODEL = 1024
PLE_DIM = 256
POOL_WINDOWS = (2, 4, 8, 16)
POOL_GROUP = D_MODEL // len(POOL_WINDOWS)
POOL_STATE = max(POOL_WINDOWS) - 1
HEAD_DIM = 64
N_HEADS = D_MODEL // HEAD_DIM
N_KV_HEADS = 4
GQA_GROUP = N_HEADS // N_KV_HEADS
KV_WIDTH = N_KV_HEADS * HEAD_DIM
WINDOW = 128
ROPE_THETA = 500000.0
ROPE_DIM = HEAD_DIM // 4
EPS = 1e-6
PAST_LEN = 16384

C_U, C_ZP, C_Q, C_K, C_V, C_ZA, C_GA, C_GB, C_END = 0, 1024, 2048, 3072, 3328, 3584, 4608, 5632, 6656

LANES = 128
SUBLANES = 8
NEG = -1e30
LOG2E = math.log2(math.e)
VMEM_LIMIT = 60 * 1024 * 1024

TQ = 512
FCH = 256
WCH = 32
WSLOTS = 6
SB = 8

F32 = jnp.float32
BF16 = jnp.bfloat16


def _sigmoid(x):
    return 1.0 / (1.0 + jnp.exp2(x * (-LOG2E)))


def _dot(a, b):
    return jnp.dot(a, b, preferred_element_type=F32)


def _dot_t(a, b):
    return lax.dot_general(a, b, (((1,), (1,)), ((), ())), preferred_element_type=F32)


def _rope(x, cos, s1, s2):
    outs = []
    for c in range(x.shape[1] // LANES):
        xc = x[:, c * LANES:(c + 1) * LANES]
        outs.append(xc * cos + pltpu.roll(xc, LANES - ROPE_DIM // 2, 1) * s1 + pltpu.roll(xc, ROPE_DIM // 2, 1) * s2)
    return outs[0] if len(outs) == 1 else jnp.concatenate(outs, axis=1)


def _head_norm(x, hn, gain):
    sq = (x * x).astype(BF16)
    ms = jnp.concatenate([_dot(sq[:, c * 256:(c + 1) * 256], hn) for c in range(x.shape[1] // 256)], axis=1)
    return x * lax.rsqrt(ms + EPS) * gain


def _rms_in(h, g):
    ms = jnp.mean(h * h, axis=-1, keepdims=True)
    return (h * lax.rsqrt(ms + EPS) * g).astype(BF16)


def _tail(h, p, ya, yb, ga, gb, wpp, wpa, wout, wg, wple):
    m = _sigmoid(ga) * _dot(ya.astype(BF16), wpp) + _sigmoid(gb) * _dot(yb.astype(BF16), wpa)
    h1 = h + _dot(m.astype(BF16), wout)
    gate = _sigmoid(_dot(h1.astype(BF16), wg))
    return h1 + gate * _dot(p.astype(BF16), wple)


def _pool_map(r, pm_ref, ps):
    rb = r.astype(BF16)
    mapped = jnp.concatenate([_dot(rb[:, g * POOL_GROUP:(g + 1) * POOL_GROUP], pm_ref[g])
                              for g in range(len(POOL_WINDOWS))], axis=1)
    return mapped * ps


def _prompt_kernel(sinks_ref, h_ref, p_ref, cos_ref, s1_ref, s2_ref, bias_ref,
                   ng_ref, gq_ref, gk_ref, ps_ref, winf_hbm, hn_ref, pm_ref,
                   wpp_ref, wpa_ref, wout_ref, wg_ref, wple_ref,
                   y_ref, nk_ref, nv_ref, npool_ref, winb_hbm,
                   kall, vall, u_s, s2_s, s4_s, s8_s, o_s, win_ref, stg, sem, *, layer):
    t = pl.program_id(1)
    first_step = jnp.logical_and(pl.program_id(0) == 0, t == 0)
    last_step = jnp.logical_and(pl.program_id(0) == pl.num_programs(0) - 1, t == pl.num_programs(1) - 1)
    nqb = TQ // WINDOW

    def stage(c):
        slot = c % WSLOTS
        return pltpu.make_async_copy(winf_hbm.at[layer, pl.ds(c * WCH, WCH), :], stg.at[slot], sem.at[slot])

    publish = pltpu.make_async_copy(win_ref, winb_hbm, sem.at[WSLOTS])

    @pl.when(first_step)
    def _():
        nch = D_MODEL // WCH
        for c in range(WSLOTS - 1):
            stage(c).start()
        for c in range(nch):
            if c + WSLOTS - 1 < nch:
                stage(c + WSLOTS - 1).start()
            stage(c).wait()
            win_ref[c * WCH:(c + 1) * WCH, :] = stg[c % WSLOTS].astype(BF16)
        publish.start()
    lane = lax.broadcasted_iota(jnp.int32, (WINDOW, LANES), 1)
    lo = lane < HEAD_DIM

    @pl.when(t == 0)
    def _():
        kall[...] = jnp.zeros_like(kall)
        vall[:, :, 0:LANES] = jnp.zeros((N_KV_HEADS, (nqb + 1) * 256, LANES), BF16)
        ones_pat = jnp.concatenate([jnp.where(lo, 1.0, 0.0), jnp.where(lo, 0.0, 1.0)], axis=0).astype(BF16)
        for kh in range(N_KV_HEADS):
            for x in range(nqb + 1):
                vall[kh, x * 256:(x + 1) * 256, LANES:2 * LANES] = ones_pat
        u_s[0:16, :] = jnp.zeros((16, D_MODEL), F32)
        s2_s[0:16, :] = jnp.zeros((16, D_MODEL), F32)
        s4_s[0:16, :] = jnp.zeros((16, 768), F32)
        s8_s[0:16, :] = jnp.zeros((16, 512), F32)

    @pl.when(t > 0)
    def _():
        for kh in range(N_KV_HEADS):
            kall[kh, 0:256, :] = kall[kh, nqb * 256:(nqb + 1) * 256, :]
            vall[kh, 0:256, 0:LANES] = vall[kh, nqb * 256:(nqb + 1) * 256, 0:LANES]

    h = h_ref[0]
    xn = _rms_in(h, ng_ref[...])

    def proj(c0, c1):
        return _dot(xn, win_ref[:, c0:c1])

    res = {}

    def fillers():
        parts = []
        for c in range(D_MODEL // FCH):
            parts.append(proj(C_U + c * FCH, C_U + (c + 1) * FCH))
            yield
        u = jnp.concatenate(parts, axis=1)
        u_s[16:16 + TQ, :] = u
        s2 = u + u_s[15:15 + TQ, :]
        s2_s[16:16 + TQ, :] = s2
        s4 = s2[:, 256:] + s2_s[14:14 + TQ, 256:]
        s4_s[16:16 + TQ, :] = s4
        s8 = s4[:, 256:] + s4_s[12:12 + TQ, 256:]
        s8_s[16:16 + TQ, :] = s8
        s16 = s8[:, 256:] + s8_s[8:8 + TQ, 256:]
        pos1 = (t * TQ + 1 + lax.broadcasted_iota(jnp.int32, (TQ, 1), 0)).astype(F32)
        wins = (s2[:, :256], s4[:, :256], s8[:, :256], s16)
        r = jnp.concatenate([wins[g] * (1.0 / jnp.minimum(pos1, float(w))) for g, w in enumerate(POOL_WINDOWS)],
                            axis=1) - u
        npool_ref[0] = u_s[TQ + 1:TQ + 16, :]
        u_s[0:16, :] = u_s[TQ:TQ + 16, :]
        s2_s[0:16, :] = s2_s[TQ:TQ + 16, :]
        s4_s[0:16, :] = s4_s[TQ:TQ + 16, :]
        s8_s[0:16, :] = s8_s[TQ:TQ + 16, :]
        mapped = _pool_map(r, pm_ref, ps_ref[...])
        yield
        for name, c_lo in (("zp", C_ZP), ("za", C_ZA), ("ga", C_GA), ("gb", C_GB)):
            parts = []
            for c in range(D_MODEL // FCH):
                parts.append(proj(c_lo + c * FCH, c_lo + (c + 1) * FCH))
                yield
            res[name] = jnp.concatenate(parts, axis=1)
        yab = (mapped * (res["zp"] * _sigmoid(res["zp"]))).astype(BF16)
        pb = p_ref[0].astype(BF16)
        for name, lhs, w_ref in (("pp", yab, wpp_ref), ("ple", pb, wple_ref)):
            parts = []
            for c in range(D_MODEL // FCH):
                parts.append(_dot(lhs, w_ref[:, c * FCH:(c + 1) * FCH]))
                yield
            res[name] = jnp.concatenate(parts, axis=1)

    def tiled_bias(b):
        b4 = jnp.concatenate([b[:, :LANES], b[:, :LANES], b[:, LANES:], b[:, LANES:]], axis=1)
        return jnp.concatenate([b4, b4], axis=0)

    biases = [tiled_bias(jnp.where(t == 0, bias_ref[1], bias_ref[0]))] + [tiled_bias(bias_ref[0])] * (nqb - 1)
    chains = [(kh, n) for kh in range(N_KV_HEADS) for n in range(nqb)]
    n_fill = 7 * (D_MODEL // FCH) + 1
    fill = fillers()

    def scores(kh, n):
        rows = slice(n * WINDOW, (n + 1) * WINDOW)
        qs = jnp.concatenate([qb[rows, kh * 256:kh * 256 + LANES],
                              qb[rows, kh * 256 + LANES:(kh + 1) * 256]], axis=0)
        return _dot_t(qs, kall[kh, n * 256:n * 256 + 512, :]) + biases[n]

    q_raw, k_raw, v = proj(C_Q, C_K), proj(C_K, C_V), proj(C_V, C_ZA)
    pre = D_MODEL // FCH + 1
    for _ in range(pre):
        next(fill)
    done = pre
    cos, s1, s2t = cos_ref[...], s1_ref[...], s2_ref[...]
    hn = hn_ref[...]
    kr = _rope(_head_norm(k_raw, hn, gk_ref[...]), cos, s1, s2t)
    qb = _rope(_head_norm(q_raw, hn, gq_ref[...]), cos, s1, s2t).astype(BF16)
    nk_ref[0] = kr[TQ - WINDOW:, :]
    nv_ref[0] = v[TQ - WINDOW:, :]
    for x in range(nqb):
        rows = slice(x * WINDOW, (x + 1) * WINDOW)
        base = (x + 1) * 256
        for pr in range(N_KV_HEADS // 2):
            for src, dst, col in ((kr, kall, None), (v, vall, slice(0, LANES))):
                a = src[rows, pr * LANES:(pr + 1) * LANES]
                ra = pltpu.roll(a, HEAD_DIM, 1)
                parts = ((jnp.where(lo, a, 0.0), jnp.where(lo, 0.0, ra)),
                         (jnp.where(lo, ra, 0.0), jnp.where(lo, 0.0, a)))
                for e in range(2):
                    kh = 2 * pr + e
                    lh = jnp.concatenate(parts[e], axis=0).astype(BF16)
                    if col is None:
                        dst[kh, base:base + 256, :] = lh
                    else:
                        dst[kh, base:base + 256, col] = lh

    s_next = scores(*chains[0])
    for ci, (kh, n) in enumerate(chains):
        s = s_next
        if ci + 1 < len(chains):
            s_next = scores(*chains[ci + 1])
        want = pre + ((n_fill - pre) * (ci + 1)) // len(chains)
        while done < want:
            next(fill)
            done += 1
        rows = slice(n * WINDOW, (n + 1) * WINDOW)
        p_rows, sink_rows = [], []
        for rr in range(2):
            rs = slice(rr * WINDOW, (rr + 1) * WINDOW)
            pcols = [None] * 4
            sterm = []
            for e in range(2):
                sink = sinks_ref[kh * GQA_GROUP + 2 * rr + e]
                sp = s[rs, e * LANES:(e + 1) * LANES]
                sc = s[rs, 256 + e * LANES:256 + (e + 1) * LANES]
                mx = jnp.maximum(jnp.max(jnp.maximum(sp, sc), axis=-1, keepdims=True), sink)
                pcols[e] = jnp.exp2(sp - mx)
                pcols[2 + e] = jnp.exp2(sc - mx)
                sterm.append(jnp.exp2(sink - mx))
            p_rows.append(jnp.concatenate(pcols, axis=1))
            sink_rows.append(jnp.where(lo, sterm[0], sterm[1]))
        pmat = jnp.concatenate(p_rows, axis=0).astype(BF16)
        o2 = _dot(pmat, vall[kh, n * 256:n * 256 + 512, :])
        o = o2[:, :LANES] / (o2[:, LANES:] + jnp.concatenate(sink_rows, axis=0))
        o_s[rows, kh * 256:kh * 256 + LANES] = o[:WINDOW]
        o_s[rows, kh * 256 + LANES:(kh + 1) * 256] = o[WINDOW:]
    for _ in fill:
        pass

    za, ga, gb = res["za"], res["ga"], res["gb"]
    yb = o_s[...] * (za * _sigmoid(za))
    halves = [slice(i * (TQ // 2), (i + 1) * (TQ // 2)) for i in range(2)]
    ybb = yb.astype(BF16)
    ap = [_dot(ybb[r], wpa_ref[...]) for r in halves]
    m = [(_sigmoid(ga[r]) * res["pp"][r] + _sigmoid(gb[r]) * ap[i]).astype(BF16) for i, r in enumerate(halves)]
    h1 = [h[r] + _dot(m[i], wout_ref[...]) for i, r in enumerate(halves)]
    g = [_dot(h1[i].astype(BF16), wg_ref[...]) for i in range(2)]
    for i, r in enumerate(halves):
        y_ref[0, r, :] = h1[i] + _sigmoid(g[i]) * res["ple"][r]

    @pl.when(last_step)
    def _():
        publish.wait()


def _const_spec(shape):
    nd = len(shape)
    return pl.BlockSpec(shape, lambda *_: (0,) * nd, pipeline_mode=pl.Buffered(1))


def _layer_spec(shape, layer):
    nd = len(shape)
    return pl.BlockSpec((None,) + tuple(shape), lambda *_: (layer,) + (0,) * nd, pipeline_mode=pl.Buffered(1))


def _weight_specs(layer, win_spec):
    ls = lambda *shape: _layer_spec(shape, layer)
    return [ls(1, D_MODEL), ls(1, D_MODEL), ls(1, KV_WIDTH), ls(1, D_MODEL),
            win_spec, _const_spec((256, 256)), ls(4, POOL_GROUP, POOL_GROUP),
            ls(D_MODEL, D_MODEL), ls(D_MODEL, D_MODEL), ls(D_MODEL, D_MODEL), ls(D_MODEL, D_MODEL), ls(PLE_DIM, D_MODEL)]


def _weight_args(w, w_in):
    return [w["norm_g"], w["gq"], w["gk"], w["pool_scale"], w_in, w["hn"], w["pool_map"],
            w["w_pp"], w["w_pa"], w["w_out"], w["w_g"], w["w_ple"]]


def _prompt_layer(h, p, tabs, bias, w, layer):
    B, T, _ = h.shape
    nt = T // TQ
    nqb = TQ // WINDOW
    row = lambda width: pl.BlockSpec((1, TQ, width), lambda b, t: (b, t, 0))
    tab = pl.BlockSpec((TQ, LANES), lambda b, t: (t, 0))
    in_specs = [
        pl.BlockSpec(memory_space=pltpu.SMEM),
        row(D_MODEL), pl.BlockSpec((None, 1, TQ, PLE_DIM), lambda b, t: (layer, b, t, 0)), tab, tab, tab,
        _const_spec(bias.shape),
    ] + _weight_specs(layer, pl.BlockSpec(memory_space=pl.ANY))
    out_specs = [
        row(D_MODEL),
        pl.BlockSpec((1, WINDOW, KV_WIDTH), lambda b, t: (b, 0, 0)),
        pl.BlockSpec((1, WINDOW, KV_WIDTH), lambda b, t: (b, 0, 0)),
        pl.BlockSpec((1, POOL_STATE, D_MODEL), lambda b, t: (b, 0, 0)),
        pl.BlockSpec(memory_space=pl.ANY),
    ]
    out_shape = [
        jax.ShapeDtypeStruct((B, T, D_MODEL), F32),
        jax.ShapeDtypeStruct((B, WINDOW, KV_WIDTH), F32),
        jax.ShapeDtypeStruct((B, WINDOW, KV_WIDTH), F32),
        jax.ShapeDtypeStruct((B, POOL_STATE, D_MODEL), F32),
        jax.ShapeDtypeStruct((D_MODEL, C_END), BF16),
    ]
    scratch = [
        pltpu.VMEM((N_KV_HEADS, (nqb + 1) * 256, LANES), BF16),
        pltpu.VMEM((N_KV_HEADS, (nqb + 1) * 256, 2 * LANES), BF16),
        pltpu.VMEM((TQ + 16, D_MODEL), F32),
        pltpu.VMEM((TQ + 16, D_MODEL), F32),
        pltpu.VMEM((TQ + 16, 768), F32),
        pltpu.VMEM((TQ + 16, 512), F32),
        pltpu.VMEM((TQ, D_MODEL), F32),
        pltpu.VMEM((D_MODEL, C_END), BF16),
        pltpu.VMEM((WSLOTS, WCH, C_END), F32),
        pltpu.SemaphoreType.DMA((WSLOTS + 1,)),
    ]
    return pl.pallas_call(
        functools.partial(_prompt_kernel, layer=layer),
        grid=(B, nt),
        in_specs=in_specs,
        out_specs=out_specs,
        out_shape=out_shape,
        scratch_shapes=scratch,
        compiler_params=pltpu.CompilerParams(dimension_semantics=("arbitrary", "arbitrary"),
                                             vmem_limit_bytes=VMEM_LIMIT),
        name="prompt_layer",
    )(w["sinks2"][layer], h, p, tabs[0], tabs[1], tabs[2], bias, *_weight_args(w, w["w_in_f32"]))


def _split3(x):
    hi = x.astype(BF16)
    r1 = x - hi.astype(F32)
    mid = r1.astype(BF16)
    return hi, mid, (r1 - mid.astype(F32)).astype(BF16)


def _sample_kernel(sinks_ref, h_ref, p_ref, st_ref, ck_ref, cv_ref, cos_ref, s1_ref, s2_ref, bmain_ref, bnew_ref, sel_ref,
                   ng_ref, gq_ref, gk_ref, ps_ref, win_ref, hn_ref, pm_ref,
                   wpp_ref, wpa_ref, wout_ref, wg_ref, wple_ref, *rest, nb, n_alias):
    y_ref, nkc_ref, nvc_ref, npool_ref, q_s, o_s, ya_s, nu_ref, nk_ref, nv_ref = rest[n_alias:]
    i = pl.program_id(0)
    nsteps = pl.num_programs(0)
    dec = h_ref.shape[0] // nb

    @pl.when(i == 0)
    def _():
        h = h_ref[...]
        xn = _rms_in(h, ng_ref[...])

        def proj(c0, c1):
            return _dot(xn, win_ref[:, c0:c1])

        nu_ref[...] = proj(C_U, C_ZP)
        zp = proj(C_ZP, C_Q)
        ya_s[...] = zp * _sigmoid(zp)

        cos, s1, s2t = cos_ref[...], s1_ref[...], s2_ref[...]
        hn = hn_ref[...]
        q_s[...] = _rope(_head_norm(proj(C_Q, C_K), hn, gq_ref[...]), cos, s1, s2t)
        nk_ref[...] = _rope(_head_norm(proj(C_K, C_V), hn, gk_ref[...]), cos, s1, s2t)
        nv_ref[...] = proj(C_V, C_ZA)

    b0 = pl.multiple_of(i * SB, SB)
    rpp = dec * SB
    nrow = N_HEADS * rpp
    lo = lax.broadcasted_iota(jnp.int32, (rpp, LANES), 1) < HEAD_DIM
    zero = jnp.zeros((rpp, LANES), F32)

    def rows_tb(ref, c0, c1):
        return jnp.concatenate([ref[pl.ds(pl.multiple_of(tt * nb + b0, SB), SB), c0:c1] for tt in range(dec)], axis=0)

    u_tb = rows_tb(nu_ref, 0, D_MODEL)
    seq = [st_ref[j] for j in range(POOL_STATE)] + [u_tb[tt * SB:(tt + 1) * SB] for tt in range(dec)]
    r_rows = []
    for tt in range(dec):
        e = POOL_STATE + tt
        cols = []
        for g, w in enumerate(POOL_WINDOWS):
            cs = slice(g * POOL_GROUP, (g + 1) * POOL_GROUP)
            acc = seq[e][:, cs]
            for d in range(1, w):
                acc = acc + seq[e - d][:, cs]
            cols.append(acc * (1.0 / w))
        r_rows.append(jnp.concatenate(cols, axis=1) - seq[e])
    ya_tb = _pool_map(jnp.concatenate(r_rows, axis=0), pm_ref, ps_ref[...])
    for tt in range(dec):
        rs = pl.ds(pl.multiple_of(tt * nb + b0, SB), SB)
        ya_s[rs, :] = ya_s[rs, :] * ya_tb[tt * SB:(tt + 1) * SB]

    slot = lax.broadcasted_iota(jnp.int32, (KV_WIDTH, WINDOW), 1)
    knew_f = rows_tb(nk_ref, 0, KV_WIDTH)
    vnew_f = rows_tb(nv_ref, 0, KV_WIDTH)

    def new_cache(src_ref, new_f, dst_ref):
        pieces3 = _split3(new_f)
        for bl in range(SB):
            cols = None
            for pc in pieces3:
                part = lax.dot_general(pc, sel_ref[bl], (((0,), (0,)), ((), ())), preferred_element_type=F32)
                cols = part if cols is None else cols + part
            dst_ref[bl] = jnp.where(slot < WINDOW - dec, pltpu.roll(src_ref[bl], WINDOW - dec, 1), cols)

    pieces = []
    for kh in range(N_KV_HEADS):
        for gp in range(GQA_GROUP // 2):
            c = kh * 2 + gp
            x = rows_tb(q_s, c * LANES, (c + 1) * LANES)
            rx = pltpu.roll(x, HEAD_DIM, 1)
            for e in range(2):
                src = x if e == kh % 2 else rx
                half = jnp.where(lo, src, 0.0) if kh % 2 == 0 else jnp.where(lo, 0.0, src)
                pieces.append(jnp.concatenate([half, zero] if kh < 2 else [zero, half], axis=1))
    lhs = jnp.concatenate(pieces, axis=0).astype(BF16)
    kmain = jnp.concatenate([ck_ref[bl].astype(BF16) for bl in range(SB)], axis=1)
    vmain = jnp.concatenate([cv_ref[bl].astype(BF16) for bl in range(SB)], axis=1)
    knew = knew_f.astype(BF16)
    vnew = vnew_f.astype(BF16)
    s_main = (_dot(lhs, kmain).reshape(N_HEADS, rpp, SB * WINDOW) + bmain_ref[...][None]).reshape(nrow, SB * WINDOW)
    s_new = (_dot_t(lhs, knew).reshape(N_HEADS, rpp, rpp) + bnew_ref[...][None]).reshape(nrow, rpp)

    new_cache(ck_ref, knew_f, nkc_ref)
    new_cache(cv_ref, vnew_f, nvc_ref)

    sink = jnp.concatenate([jnp.full((rpp, 1), sinks_ref[hh], F32) for hh in range(N_HEADS)], axis=0)
    mx = jnp.maximum(jnp.maximum(jnp.max(s_main, axis=-1, keepdims=True), jnp.max(s_new, axis=-1, keepdims=True)), sink)
    p_main = jnp.exp2(s_main - mx).astype(BF16)
    p_new = jnp.exp2(s_new - mx).astype(BF16)
    den = (jnp.sum(p_main.astype(F32), axis=-1, keepdims=True) + jnp.sum(p_new.astype(F32), axis=-1, keepdims=True)
           + jnp.exp2(sink - mx))
    o = (_dot_t(p_main, vmain) + _dot(p_new, vnew)) / den
    for kh in range(N_KV_HEADS):
        for gp in range(GQA_GROUP // 2):
            halves = []
            for e in range(2):
                hh = kh * GQA_GROUP + 2 * gp + e
                x = o[hh * rpp:(hh + 1) * rpp, (kh // 2) * LANES:(kh // 2 + 1) * LANES]
                halves.append(x if e == kh % 2 else pltpu.roll(x, HEAD_DIM, 1))
            dest = jnp.where(lo, halves[0], halves[1])
            c = kh * 2 + gp
            for tt in range(dec):
                o_s[pl.ds(pl.multiple_of(tt * nb + b0, SB), SB), c * LANES:(c + 1) * LANES] = dest[tt * SB:(tt + 1) * SB]

    for j in range(POOL_STATE):
        npool_ref[j] = seq[j + dec]

    @pl.when(i == nsteps - 1)
    def _():
        h = h_ref[...]
        xn = _rms_in(h, ng_ref[...])
        za = _dot(xn, win_ref[:, C_ZA:C_GA])
        yb = o_s[...] * (za * _sigmoid(za))
        y_ref[...] = _tail(h, p_ref[...], ya_s[...], yb, _dot(xn, win_ref[:, C_GA:C_GB]), _dot(xn, win_ref[:, C_GB:C_END]),
                           wpp_ref[...], wpa_ref[...], wout_ref[...], wg_ref[...], wple_ref[...])


def _sample_layer(h, p, st, ck, cv, tabs, bmain, bnew, sel, w, w_in_bf, layer, stacked):
    R = h.shape[0]
    nb = ck.shape[1]
    full = lambda shape: pl.BlockSpec(shape, lambda i: (0,) * len(shape), pipeline_mode=pl.Buffered(1))
    cache = pl.BlockSpec((None, SB, KV_WIDTH, WINDOW), lambda i: (layer, i, 0, 0))
    state = pl.BlockSpec((None, POOL_STATE, SB, D_MODEL), lambda i: (layer, 0, i, 0))
    in_specs = [
        pl.BlockSpec(memory_space=pltpu.SMEM),
        full((R, D_MODEL)), full((R, PLE_DIM)), state, cache, cache,
        full((R, LANES)), full((R, LANES)), full((R, LANES)), full(bmain.shape), full(bnew.shape), full(sel.shape),
    ] + _weight_specs(layer, full((D_MODEL, C_END)))
    out_specs = [full((R, D_MODEL)), cache, cache, state]
    out_shape = [jax.ShapeDtypeStruct((R, D_MODEL), F32), jax.ShapeDtypeStruct(ck.shape, F32),
                 jax.ShapeDtypeStruct(cv.shape, F32), jax.ShapeDtypeStruct(st.shape, F32)]
    scratch = [pltpu.VMEM((R, D_MODEL), F32) for _ in range(4)] + [pltpu.VMEM((R, KV_WIDTH), F32) for _ in range(2)]
    args = [w["sinks2"][layer], h, p, st, ck, cv, tabs[0], tabs[1], tabs[2], bmain, bnew, sel] + _weight_args(w, w_in_bf)
    aliases = {}
    if stacked is not None:
        aliases = {len(args) + k: 1 + k for k in range(len(stacked))}
        args += list(stacked)
        in_specs += [pl.BlockSpec(memory_space=pl.ANY)] * len(stacked)
    return pl.pallas_call(
        functools.partial(_sample_kernel, nb=nb, n_alias=len(aliases)),
        grid=(nb // SB,),
        in_specs=in_specs,
        out_specs=out_specs,
        out_shape=out_shape,
        scratch_shapes=scratch,
        input_output_aliases=aliases,
        compiler_params=pltpu.CompilerParams(dimension_semantics=("arbitrary",), vmem_limit_bytes=VMEM_LIMIT),
        name="sample_layer",
    )(*args)


def _rope_tables(pos):
    pos = np.asarray(pos, np.float64)
    inv_freq = ROPE_THETA ** (-np.arange(0, ROPE_DIM, 2, dtype=np.float64) / ROPE_DIM)
    ang = pos[:, None] * inv_freq[None, :]
    cos, sin = np.cos(ang), np.sin(ang)
    n = pos.shape[0]
    half = ROPE_DIM // 2
    c64 = np.concatenate([cos, cos, np.ones((n, HEAD_DIM - ROPE_DIM))], axis=1)
    a64 = np.concatenate([-sin, np.zeros((n, HEAD_DIM - half))], axis=1)
    b64 = np.concatenate([np.zeros((n, half)), sin, np.zeros((n, HEAD_DIM - ROPE_DIM))], axis=1)
    rep = LANES // HEAD_DIM
    return tuple(jnp.asarray(np.tile(x, (1, rep)), F32) for x in (c64, a64, b64))


def _prompt_bias():
    q = np.arange(WINDOW)[:, None]
    j = np.arange(WINDOW)[None, :]
    prev = np.where(j >= q, 0.0, NEG)
    cur = np.where(j <= q, 0.0, NEG)
    normal = np.concatenate([prev, cur], axis=1)
    first = np.concatenate([np.full_like(prev, NEG), cur], axis=1)
    return jnp.asarray(np.stack([normal, first]), F32)


def _sample_bias(dec):
    rq = np.arange(dec * SB)
    tq, bq = rq // SB, rq % SB
    cm = np.arange(SB * WINDOW)
    bm, jm = cm // WINDOW, cm % WINDOW
    main = np.where((bq[:, None] == bm[None, :]) & (jm[None, :] >= tq[:, None]), 0.0, NEG)
    new = np.where((bq[:, None] == bq[None, :]) & (tq[None, :] <= tq[:, None]), 0.0, NEG)
    sel = np.zeros((SB, dec * SB, WINDOW))
    for r in rq:
        sel[bq[r], r, WINDOW - dec + tq[r]] = 1.0
    return jnp.asarray(main, F32), jnp.asarray(new, F32), jnp.asarray(sel, BF16)


def _prep_weights(norm_g, w_in, q_norm_g, k_norm_g, sinks, pool_map, pool_scale,
                  w_proj_pool, w_proj_attn, w_out, w_ple, w_ple_gate):
    qscale = HEAD_DIM ** -0.5 * LOG2E
    blk = np.kron(np.eye(256 // HEAD_DIM), np.full((HEAD_DIM, HEAD_DIM), 1.0 / HEAD_DIM))
    return dict(
        sinks2=sinks * LOG2E,
        norm_g=norm_g[:, None, :],
        gq=jnp.tile(q_norm_g * qscale, (1, N_HEADS))[:, None, :],
        gk=jnp.tile(k_norm_g, (1, N_KV_HEADS))[:, None, :],
        pool_scale=pool_scale[:, None, :],
        w_in_f32=w_in,
        hn=jnp.asarray(blk, BF16),
        pool_map=pool_map.astype(BF16),
        w_pp=w_proj_pool.astype(BF16), w_pa=w_proj_attn.astype(BF16),
        w_out=w_out.astype(BF16), w_g=w_ple_gate.astype(BF16), w_ple=w_ple.astype(BF16),
    )


def kernel(x_prompt, x_sample, cache_k, cache_v, state_pool, p_prompt, p_sample, norm_g, w_in, q_norm_g, k_norm_g, sinks, pool_map, pool_scale, w_proj_pool, w_proj_attn, w_out, w_ple, w_ple_gate):
    depth = w_in.shape[0]
    B, T, _ = x_prompt.shape
    nb, dec, _ = x_sample.shape
    assert T % TQ == 0 and TQ % WINDOW == 0 and nb % SB == 0
    ptabs = _rope_tables(np.arange(T))
    stabs = _rope_tables(np.repeat(PAST_LEN + np.arange(dec), nb))
    pbias = _prompt_bias()
    bmain, bnew, sel = _sample_bias(dec)

    hp = x_prompt
    hs = x_sample.transpose(1, 0, 2).reshape(dec * nb, D_MODEL)
    ck_all = cache_k.transpose(0, 1, 3, 4, 2).reshape(depth, nb, KV_WIDTH, WINDOW)
    cv_all = cache_v.transpose(0, 1, 3, 4, 2).reshape(depth, nb, KV_WIDTH, WINDOW)
    st_all = state_pool.transpose(0, 2, 1, 3)
    stacked = None
    kp_l, vp_l, pp_l = [], [], []
    w = _prep_weights(norm_g, w_in, q_norm_g, k_norm_g, sinks, pool_map, pool_scale,
                      w_proj_pool, w_proj_attn, w_out, w_ple, w_ple_gate)
    for i in range(depth):
        hp, nk, nv, npool, w_in_bf = _prompt_layer(hp, p_prompt, ptabs, pbias, w, i)
        kp_l.append(nk.reshape(B, WINDOW, N_KV_HEADS, HEAD_DIM))
        vp_l.append(nv.reshape(B, WINDOW, N_KV_HEADS, HEAD_DIM))
        pp_l.append(npool)

        ps_t = p_sample[i].transpose(1, 0, 2).reshape(dec * nb, PLE_DIM)
        hs, *stacked = _sample_layer(hs, ps_t, st_all, ck_all, cv_all, stabs, bmain, bnew, sel, w, w_in_bf, i, stacked)
    uncache = lambda a: a.reshape(depth, nb, N_KV_HEADS, HEAD_DIM, WINDOW).transpose(0, 1, 4, 2, 3)
    new_k_sample, new_v_sample = uncache(stacked[0]), uncache(stacked[1])
    new_pool_sample = stacked[2].transpose(0, 2, 1, 3)
    y_sample = hs.reshape(dec, nb, D_MODEL).transpose(1, 0, 2)
    return (hp, y_sample, jnp.stack(kp_l), jnp.stack(vp_l), jnp.stack(pp_l),
            new_k_sample, new_v_sample, new_pool_sample)
```

```python
import functools
import math

import numpy as np
import jax
import jax.numpy as jnp
from jax import lax
from jax.experimental import pallas as pl
from jax.experimental.pallas import tpu as pltpu

D_MODEL = 1024
PLE_DIM = 256
POOL_WINDOWS = (2, 4, 8, 16)
POOL_GROUP = D_MODEL // len(POOL_WINDOWS)
POOL_STATE = max(POOL_WINDOWS) - 1
HEAD_DIM = 64
N_HEADS = D_MODEL // HEAD_DIM
N_KV_HEADS = 4
GQA_GROUP = N_HEADS // N_KV_HEADS
KV_WIDTH = N_KV_HEADS * HEAD_DIM
WINDOW = 128
ROPE_THETA = 500000.0
ROPE_DIM = HEAD_DIM // 4
EPS = 1e-6
PAST_LEN = 16384

C_U, C_ZP, C_Q, C_K, C_V, C_ZA, C_GA, C_GB, C_END = 0, 1024, 2048, 3072, 3328, 3584, 4608, 5632, 6656

LANES = 128
SUBLANES = 8
NEG = -1e30
LOG2E = math.log2(math.e)
VMEM_LIMIT = 60 * 1024 * 1024

TQ = 512
FCH = 256
WCH = 16
WSLOTS = 12
SB = 8

F32 = jnp.float32
BF16 = jnp.bfloat16


def _sigmoid(x):
    return 1.0 / (1.0 + jnp.exp2(x * (-LOG2E)))


def _dot(a, b):
    return jnp.dot(a, b, preferred_element_type=F32)


def _dot_t(a, b):
    return lax.dot_general(a, b, (((1,), (1,)), ((), ())), preferred_element_type=F32)


def _rope(x, cos, s1, s2):
    outs = []
    for c in range(x.shape[1] // LANES):
        xc = x[:, c * LANES:(c + 1) * LANES]
        outs.append(xc * cos + pltpu.roll(xc, LANES - ROPE_DIM // 2, 1) * s1 + pltpu.roll(xc, ROPE_DIM // 2, 1) * s2)
    return outs[0] if len(outs) == 1 else jnp.concatenate(outs, axis=1)


def _head_norm(x, hn, gain):
    sq = (x * x).astype(BF16)
    ms = jnp.concatenate([_dot(sq[:, c * 256:(c + 1) * 256], hn) for c in range(x.shape[1] // 256)], axis=1)
    return x * lax.rsqrt(ms + EPS) * gain


def _rms_in(h, g):
    ms = jnp.mean(h * h, axis=-1, keepdims=True)
    return (h * lax.rsqrt(ms + EPS) * g).astype(BF16)


def _tail(h, p, ya, yb, ga, gb, wpp, wpa, wout, wg, wple):
    m = _sigmoid(ga) * _dot(ya.astype(BF16), wpp) + _sigmoid(gb) * _dot(yb.astype(BF16), wpa)
    h1 = h + _dot(m.astype(BF16), wout)
    gate = _sigmoid(_dot(h1.astype(BF16), wg))
    return h1 + gate * _dot(p.astype(BF16), wple)


def _pool_map(r, pm_ref, ps):
    rb = r.astype(BF16)
    mapped = jnp.concatenate([_dot(rb[:, g * POOL_GROUP:(g + 1) * POOL_GROUP], pm_ref[g])
                              for g in range(len(POOL_WINDOWS))], axis=1)
    return mapped * ps


def _prompt_kernel(sinks_ref, h_ref, p_ref, cos_ref, s1_ref, s2_ref, bias_ref,
                   ng_ref, gq_ref, gk_ref, ps_ref, winf_hbm, hn_ref, pm_ref,
                   wpp_ref, wpa_ref, wout_ref, wg_ref, wple_ref,
                   y_ref, nk_ref, nv_ref, npool_ref, winb_hbm,
                   kall, vall, u_s, s2_s, s4_s, s8_s, o_s, win_ref, stg, sem, *, layer):
    t = pl.program_id(1)
    first_step = jnp.logical_and(pl.program_id(0) == 0, t == 0)
    last_step = jnp.logical_and(pl.program_id(0) == pl.num_programs(0) - 1, t == pl.num_programs(1) - 1)
    nqb = TQ // WINDOW

    def stage(c):
        slot = c % WSLOTS
        return pltpu.make_async_copy(winf_hbm.at[layer, pl.ds(c * WCH, WCH), :], stg.at[slot], sem.at[slot])

    publish = pltpu.make_async_copy(win_ref, winb_hbm, sem.at[WSLOTS])

    @pl.when(first_step)
    def _():
        nch = D_MODEL // WCH
        for c in range(WSLOTS - 1):
            stage(c).start()
        for c in range(nch):
            if c + WSLOTS - 1 < nch:
                stage(c + WSLOTS - 1).start()
            stage(c).wait()
            win_ref[c * WCH:(c + 1) * WCH, :] = stg[c % WSLOTS].astype(BF16)
        publish.start()
    lane = lax.broadcasted_iota(jnp.int32, (WINDOW, LANES), 1)
    lo = lane < HEAD_DIM

    @pl.when(t == 0)
    def _():
        kall[...] = jnp.zeros_like(kall)
        vall[:, :, 0:LANES] = jnp.zeros((N_KV_HEADS, (nqb + 1) * 256, LANES), BF16)
        ones_pat = jnp.concatenate([jnp.where(lo, 1.0, 0.0), jnp.where(lo, 0.0, 1.0)], axis=0).astype(BF16)
        for kh in range(N_KV_HEADS):
            for x in range(nqb + 1):
                vall[kh, x * 256:(x + 1) * 256, LANES:2 * LANES] = ones_pat
        u_s[0:16, :] = jnp.zeros((16, D_MODEL), F32)
        s2_s[0:16, :] = jnp.zeros((16, D_MODEL), F32)
        s4_s[0:16, :] = jnp.zeros((16, 768), F32)
        s8_s[0:16, :] = jnp.zeros((16, 512), F32)

    @pl.when(t > 0)
    def _():
        for kh in range(N_KV_HEADS):
            kall[kh, 0:256, :] = kall[kh, nqb * 256:(nqb + 1) * 256, :]
            vall[kh, 0:256, 0:LANES] = vall[kh, nqb * 256:(nqb + 1) * 256, 0:LANES]

    h = h_ref[0]
    xn = _rms_in(h, ng_ref[...])

    def proj(c0, c1):
        return _dot(xn, win_ref[:, c0:c1])

    res = {}

    def fillers():
        parts = []
        for c in range(D_MODEL // FCH):
            parts.append(proj(C_U + c * FCH, C_U + (c + 1) * FCH))
            yield
        u = jnp.concatenate(parts, axis=1)
        u_s[16:16 + TQ, :] = u
        s2 = u + u_s[15:15 + TQ, :]
        s2_s[16:16 + TQ, :] = s2
        s4 = s2[:, 256:] + s2_s[14:14 + TQ, 256:]
        s4_s[16:16 + TQ, :] = s4
        s8 = s4[:, 256:] + s4_s[12:12 + TQ, 256:]
        s8_s[16:16 + TQ, :] = s8
        s16 = s8[:, 256:] + s8_s[8:8 + TQ, 256:]
        pos1 = (t * TQ + 1 + lax.broadcasted_iota(jnp.int32, (TQ, 1), 0)).astype(F32)
        wins = (s2[:, :256], s4[:, :256], s8[:, :256], s16)
        r = jnp.concatenate([wins[g] * (1.0 / jnp.minimum(pos1, float(w))) for g, w in enumerate(POOL_WINDOWS)],
                            axis=1) - u
        npool_ref[0] = u_s[TQ + 1:TQ + 16, :]
        u_s[0:16, :] = u_s[TQ:TQ + 16, :]
        s2_s[0:16, :] = s2_s[TQ:TQ + 16, :]
        s4_s[0:16, :] = s4_s[TQ:TQ + 16, :]
        s8_s[0:16, :] = s8_s[TQ:TQ + 16, :]
        mapped = _pool_map(r, pm_ref, ps_ref[...])
        yield
        for name, c_lo in (("zp", C_ZP), ("za", C_ZA), ("ga", C_GA), ("gb", C_GB)):
            parts = []
            for c in range(D_MODEL // FCH):
                parts.append(proj(c_lo + c * FCH, c_lo + (c + 1) * FCH))
                yield
            res[name] = jnp.concatenate(parts, axis=1)
        yab = (mapped * (res["zp"] * _sigmoid(res["zp"]))).astype(BF16)
        pb = p_ref[0].astype(BF16)
        for name, lhs, w_ref in (("pp", yab, wpp_ref), ("ple", pb, wple_ref)):
            parts = []
            for c in range(D_MODEL // FCH):
                parts.append(_dot(lhs, w_ref[:, c * FCH:(c + 1) * FCH]))
                yield
            res[name] = jnp.concatenate(parts, axis=1)

    def tiled_bias(b):
        b4 = jnp.concatenate([b[:, :LANES], b[:, :LANES], b[:, LANES:], b[:, LANES:]], axis=1)
        return jnp.concatenate([b4, b4], axis=0)

    biases = [tiled_bias(jnp.where(t == 0, bias_ref[1], bias_ref[0]))] + [tiled_bias(bias_ref[0])] * (nqb - 1)
    chains = [(kh, n) for kh in range(N_KV_HEADS) for n in range(nqb)]
    n_fill = 7 * (D_MODEL // FCH) + 1
    fill = fillers()

    def scores(kh, n):
        rows = slice(n * WINDOW, (n + 1) * WINDOW)
        qs = jnp.concatenate([qb[rows, kh * 256:kh * 256 + LANES],
                              qb[rows, kh * 256 + LANES:(kh + 1) * 256]], axis=0)
        return _dot_t(qs, kall[kh, n * 256:n * 256 + 512, :]) + biases[n]

    q_raw, k_raw, v = proj(C_Q, C_K), proj(C_K, C_V), proj(C_V, C_ZA)
    pre = D_MODEL // FCH + 1
    for _ in range(pre):
        next(fill)
    done = pre
    cos, s1, s2t = cos_ref[...], s1_ref[...], s2_ref[...]
    hn = hn_ref[...]
    kr = _rope(_head_norm(k_raw, hn, gk_ref[...]), cos, s1, s2t)
    qb = _rope(_head_norm(q_raw, hn, gq_ref[...]), cos, s1, s2t).astype(BF16)
    nk_ref[0] = kr[TQ - WINDOW:, :]
    nv_ref[0] = v[TQ - WINDOW:, :]
    for x in range(nqb):
        rows = slice(x * WINDOW, (x + 1) * WINDOW)
        base = (x + 1) * 256
        for pr in range(N_KV_HEADS // 2):
            for src, dst, col in ((kr, kall, None), (v, vall, slice(0, LANES))):
                a = src[rows, pr * LANES:(pr + 1) * LANES]
                ra = pltpu.roll(a, HEAD_DIM, 1)
                parts = ((jnp.where(lo, a, 0.0), jnp.where(lo, 0.0, ra)),
                         (jnp.where(lo, ra, 0.0), jnp.where(lo, 0.0, a)))
                for e in range(2):
                    kh = 2 * pr + e
                    lh = jnp.concatenate(parts[e], axis=0).astype(BF16)
                    if col is None:
                        dst[kh, base:base + 256, :] = lh
                    else:
                        dst[kh, base:base + 256, col] = lh

    s_next = scores(*chains[0])
    for ci, (kh, n) in enumerate(chains):
        s = s_next
        if ci + 1 < len(chains):
            s_next = scores(*chains[ci + 1])
        want = pre + ((n_fill - pre) * (ci + 1)) // len(chains)
        while done < want:
            next(fill)
            done += 1
        rows = slice(n * WINDOW, (n + 1) * WINDOW)
        p_rows, sink_rows = [], []
        for rr in range(2):
            rs = slice(rr * WINDOW, (rr + 1) * WINDOW)
            pcols = [None] * 4
            sterm = []
            for e in range(2):
                sink = sinks_ref[kh * GQA_GROUP + 2 * rr + e]
                sp = s[rs, e * LANES:(e + 1) * LANES]
                sc = s[rs, 256 + e * LANES:256 + (e + 1) * LANES]
                mx = jnp.maximum(jnp.max(jnp.maximum(sp, sc), axis=-1, keepdims=True), sink)
                pcols[e] = jnp.exp2(sp - mx)
                pcols[2 + e] = jnp.exp2(sc - mx)
                sterm.append(jnp.exp2(sink - mx))
            p_rows.append(jnp.concatenate(pcols, axis=1))
            sink_rows.append(jnp.where(lo, sterm[0], sterm[1]))
        pmat = jnp.concatenate(p_rows, axis=0).astype(BF16)
        o2 = _dot(pmat, vall[kh, n * 256:n * 256 + 512, :])
        o = o2[:, :LANES] / (o2[:, LANES:] + jnp.concatenate(sink_rows, axis=0))
        o_s[rows, kh * 256:kh * 256 + LANES] = o[:WINDOW]
        o_s[rows, kh * 256 + LANES:(kh + 1) * 256] = o[WINDOW:]
    for _ in fill:
        pass

    za, ga, gb = res["za"], res["ga"], res["gb"]
    yb = o_s[...] * (za * _sigmoid(za))
    halves = [slice(i * (TQ // 2), (i + 1) * (TQ // 2)) for i in range(2)]
    ybb = yb.astype(BF16)
    ap = [_dot(ybb[r], wpa_ref[...]) for r in halves]
    m = [(_sigmoid(ga[r]) * res["pp"][r] + _sigmoid(gb[r]) * ap[i]).astype(BF16) for i, r in enumerate(halves)]
    h1 = [h[r] + _dot(m[i], wout_ref[...]) for i, r in enumerate(halves)]
    g = [_dot(h1[i].astype(BF16), wg_ref[...]) for i in range(2)]
    for i, r in enumerate(halves):
        y_ref[0, r, :] = h1[i] + _sigmoid(g[i]) * res["ple"][r]

    @pl.when(last_step)
    def _():
        publish.wait()


def _const_spec(shape):
    nd = len(shape)
    return pl.BlockSpec(shape, lambda *_: (0,) * nd, pipeline_mode=pl.Buffered(1))


def _layer_spec(shape, layer):
    nd = len(shape)
    return pl.BlockSpec((None,) + tuple(shape), lambda *_: (layer,) + (0,) * nd, pipeline_mode=pl.Buffered(1))


def _weight_specs(layer, win_spec):
    ls = lambda *shape: _layer_spec(shape, layer)
    return [ls(1, D_MODEL), ls(1, D_MODEL), ls(1, KV_WIDTH), ls(1, D_MODEL),
            win_spec, _const_spec((256, 256)), ls(4, POOL_GROUP, POOL_GROUP),
            ls(D_MODEL, D_MODEL), ls(D_MODEL, D_MODEL), ls(D_MODEL, D_MODEL), ls(D_MODEL, D_MODEL), ls(PLE_DIM, D_MODEL)]


def _weight_args(w, w_in):
    return [w["norm_g"], w["gq"], w["gk"], w["pool_scale"], w_in, w["hn"], w["pool_map"],
            w["w_pp"], w["w_pa"], w["w_out"], w["w_g"], w["w_ple"]]


def _prompt_layer(h, p, tabs, bias, w, layer):
    B, T, _ = h.shape
    nt = T // TQ
    nqb = TQ // WINDOW
    row = lambda width: pl.BlockSpec((1, TQ, width), lambda b, t: (b, t, 0))
    tab = pl.BlockSpec((TQ, LANES), lambda b, t: (t, 0))
    in_specs = [
        pl.BlockSpec(memory_space=pltpu.SMEM),
        row(D_MODEL), pl.BlockSpec((None, 1, TQ, PLE_DIM), lambda b, t: (layer, b, t, 0)), tab, tab, tab,
        _const_spec(bias.shape),
    ] + _weight_specs(layer, pl.BlockSpec(memory_space=pl.ANY))
    out_specs = [
        row(D_MODEL),
        pl.BlockSpec((1, WINDOW, KV_WIDTH), lambda b, t: (b, 0, 0)),
        pl.BlockSpec((1, WINDOW, KV_WIDTH), lambda b, t: (b, 0, 0)),
        pl.BlockSpec((1, POOL_STATE, D_MODEL), lambda b, t: (b, 0, 0)),
        pl.BlockSpec(memory_space=pl.ANY),
    ]
    out_shape = [
        jax.ShapeDtypeStruct((B, T, D_MODEL), F32),
        jax.ShapeDtypeStruct((B, WINDOW, KV_WIDTH), F32),
        jax.ShapeDtypeStruct((B, WINDOW, KV_WIDTH), F32),
        jax.ShapeDtypeStruct((B, POOL_STATE, D_MODEL), F32),
        jax.ShapeDtypeStruct((D_MODEL, C_END), BF16),
    ]
    scratch = [
        pltpu.VMEM((N_KV_HEADS, (nqb + 1) * 256, LANES), BF16),
        pltpu.VMEM((N_KV_HEADS, (nqb + 1) * 256, 2 * LANES), BF16),
        pltpu.VMEM((TQ + 16, D_MODEL), F32),
        pltpu.VMEM((TQ + 16, D_MODEL), F32),
        pltpu.VMEM((TQ + 16, 768), F32),
        pltpu.VMEM((TQ + 16, 512), F32),
        pltpu.VMEM((TQ, D_MODEL), F32),
        pltpu.VMEM((D_MODEL, C_END), BF16),
        pltpu.VMEM((WSLOTS, WCH, C_END), F32),
        pltpu.SemaphoreType.DMA((WSLOTS + 1,)),
    ]
    return pl.pallas_call(
        functools.partial(_prompt_kernel, layer=layer),
        grid=(B, nt),
        in_specs=in_specs,
        out_specs=out_specs,
        out_shape=out_shape,
        scratch_shapes=scratch,
        compiler_params=pltpu.CompilerParams(dimension_semantics=("arbitrary", "arbitrary"),
                                             vmem_limit_bytes=VMEM_LIMIT),
        name="prompt_layer",
    )(w["sinks2"][layer], h, p, tabs[0], tabs[1], tabs[2], bias, *_weight_args(w, w["w_in_f32"]))


def _split3(x):
    hi = x.astype(BF16)
    r1 = x - hi.astype(F32)
    mid = r1.astype(BF16)
    return hi, mid, (r1 - mid.astype(F32)).astype(BF16)


def _sample_kernel(sinks_ref, h_ref, p_ref, st_ref, ck_ref, cv_ref, cos_ref, s1_ref, s2_ref, bmain_ref, bnew_ref, sel_ref,
                   ng_ref, gq_ref, gk_ref, ps_ref, win_ref, hn_ref, pm_ref,
                   wpp_ref, wpa_ref, wout_ref, wg_ref, wple_ref, *rest, nb, n_alias):
    y_ref, nkc_ref, nvc_ref, npool_ref, q_s, o_s, ya_s, nu_ref, nk_ref, nv_ref = rest[n_alias:]
    i = pl.program_id(0)
    nsteps = pl.num_programs(0)
    dec = h_ref.shape[0] // nb

    @pl.when(i == 0)
    def _():
        h = h_ref[...]
        xn = _rms_in(h, ng_ref[...])

        def proj(c0, c1):
            return _dot(xn, win_ref[:, c0:c1])

        nu_ref[...] = proj(C_U, C_ZP)
        zp = proj(C_ZP, C_Q)
        ya_s[...] = zp * _sigmoid(zp)

        cos, s1, s2t = cos_ref[...], s1_ref[...], s2_ref[...]
        hn = hn_ref[...]
        q_s[...] = _rope(_head_norm(proj(C_Q, C_K), hn, gq_ref[...]), cos, s1, s2t)
        nk_ref[...] = _rope(_head_norm(proj(C_K, C_V), hn, gk_ref[...]), cos, s1, s2t)
        nv_ref[...] = proj(C_V, C_ZA)

    b0 = pl.multiple_of(i * SB, SB)
    rpp = dec * SB
    nrow = N_HEADS * rpp
    lo = lax.broadcasted_iota(jnp.int32, (rpp, LANES), 1) < HEAD_DIM
    zero = jnp.zeros((rpp, LANES), F32)

    def rows_tb(ref, c0, c1):
        return jnp.concatenate([ref[pl.ds(pl.multiple_of(tt * nb + b0, SB), SB), c0:c1] for tt in range(dec)], axis=0)

    u_tb = rows_tb(nu_ref, 0, D_MODEL)
    seq = [st_ref[j] for j in range(POOL_STATE)] + [u_tb[tt * SB:(tt + 1) * SB] for tt in range(dec)]
    r_rows = []
    for tt in range(dec):
        e = POOL_STATE + tt
        cols = []
        for g, w in enumerate(POOL_WINDOWS):
            cs = slice(g * POOL_GROUP, (g + 1) * POOL_GROUP)
            acc = seq[e][:, cs]
            for d in range(1, w):
                acc = acc + seq[e - d][:, cs]
            cols.append(acc * (1.0 / w))
        r_rows.append(jnp.concatenate(cols, axis=1) - seq[e])
    ya_tb = _pool_map(jnp.concatenate(r_rows, axis=0), pm_ref, ps_ref[...])
    for tt in range(dec):
        rs = pl.ds(pl.multiple_of(tt * nb + b0, SB), SB)
        ya_s[rs, :] = ya_s[rs, :] * ya_tb[tt * SB:(tt + 1) * SB]

    slot = lax.broadcasted_iota(jnp.int32, (KV_WIDTH, WINDOW), 1)
    knew_f = rows_tb(nk_ref, 0, KV_WIDTH)
    vnew_f = rows_tb(nv_ref, 0, KV_WIDTH)

    def new_cache(src_ref, new_f, dst_ref):
        pieces3 = _split3(new_f)
        for bl in range(SB):
            cols = None
            for pc in pieces3:
                part = lax.dot_general(pc, sel_ref[bl], (((0,), (0,)), ((), ())), preferred_element_type=F32)
                cols = part if cols is None else cols + part
            dst_ref[bl] = jnp.where(slot < WINDOW - dec, pltpu.roll(src_ref[bl], WINDOW - dec, 1), cols)

    pieces = []
    for kh in range(N_KV_HEADS):
        for gp in range(GQA_GROUP // 2):
            c = kh * 2 + gp
            x = rows_tb(q_s, c * LANES, (c + 1) * LANES)
            rx = pltpu.roll(x, HEAD_DIM, 1)
            for e in range(2):
                src = x if e == kh % 2 else rx
                half = jnp.where(lo, src, 0.0) if kh % 2 == 0 else jnp.where(lo, 0.0, src)
                pieces.append(jnp.concatenate([half, zero] if kh < 2 else [zero, half], axis=1))
    lhs = jnp.concatenate(pieces, axis=0).astype(BF16)
    kmain = jnp.concatenate([ck_ref[bl].astype(BF16) for bl in range(SB)], axis=1)
    vmain = jnp.concatenate([cv_ref[bl].astype(BF16) for bl in range(SB)], axis=1)
    knew = knew_f.astype(BF16)
    vnew = vnew_f.astype(BF16)
    s_main = (_dot(lhs, kmain).reshape(N_HEADS, rpp, SB * WINDOW) + bmain_ref[...][None]).reshape(nrow, SB * WINDOW)
    s_new = (_dot_t(lhs, knew).reshape(N_HEADS, rpp, rpp) + bnew_ref[...][None]).reshape(nrow, rpp)

    new_cache(ck_ref, knew_f, nkc_ref)
    new_cache(cv_ref, vnew_f, nvc_ref)

    sink = jnp.concatenate([jnp.full((rpp, 1), sinks_ref[hh], F32) for hh in range(N_HEADS)], axis=0)
    mx = jnp.maximum(jnp.maximum(jnp.max(s_main, axis=-1, keepdims=True), jnp.max(s_new, axis=-1, keepdims=True)), sink)
    p_main = jnp.exp2(s_main - mx).astype(BF16)
    p_new = jnp.exp2(s_new - mx).astype(BF16)
    den = (jnp.sum(p_main.astype(F32), axis=-1, keepdims=True) + jnp.sum(p_new.astype(F32), axis=-1, keepdims=True)
           + jnp.exp2(sink - mx))
    o = (_dot_t(p_main, vmain) + _dot(p_new, vnew)) / den
    for kh in range(N_KV_HEADS):
        for gp in range(GQA_GROUP // 2):
            halves = []
            for e in range(2):
                hh = kh * GQA_GROUP + 2 * gp + e
                x = o[hh * rpp:(hh + 1) * rpp, (kh // 2) * LANES:(kh // 2 + 1) * LANES]
                halves.append(x if e == kh % 2 else pltpu.roll(x, HEAD_DIM, 1))
            dest = jnp.where(lo, halves[0], halves[1])
            c = kh * 2 + gp
            for tt in range(dec):
                o_s[pl.ds(pl.multiple_of(tt * nb + b0, SB), SB), c * LANES:(c + 1) * LANES] = dest[tt * SB:(tt + 1) * SB]

    for j in range(POOL_STATE):
        npool_ref[j] = seq[j + dec]

    @pl.when(i == nsteps - 1)
    def _():
        h = h_ref[...]
        xn = _rms_in(h, ng_ref[...])
        za = _dot(xn, win_ref[:, C_ZA:C_GA])
        yb = o_s[...] * (za * _sigmoid(za))
        y_ref[...] = _tail(h, p_ref[...], ya_s[...], yb, _dot(xn, win_ref[:, C_GA:C_GB]), _dot(xn, win_ref[:, C_GB:C_END]),
                           wpp_ref[...], wpa_ref[...], wout_ref[...], wg_ref[...], wple_ref[...])


def _sample_layer(h, p, st, ck, cv, tabs, bmain, bnew, sel, w, w_in_bf, layer, stacked):
    R = h.shape[0]
    nb = ck.shape[1]
    full = lambda shape: pl.BlockSpec(shape, lambda i: (0,) * len(shape), pipeline_mode=pl.Buffered(1))
    cache = pl.BlockSpec((None, SB, KV_WIDTH, WINDOW), lambda i: (layer, i, 0, 0))
    state = pl.BlockSpec((None, POOL_STATE, SB, D_MODEL), lambda i: (layer, 0, i, 0))
    in_specs = [
        pl.BlockSpec(memory_space=pltpu.SMEM),
        full((R, D_MODEL)), full((R, PLE_DIM)), state, cache, cache,
        full((R, LANES)), full((R, LANES)), full((R, LANES)), full(bmain.shape), full(bnew.shape), full(sel.shape),
    ] + _weight_specs(layer, full((D_MODEL, C_END)))
    out_specs = [full((R, D_MODEL)), cache, cache, state]
    out_shape = [jax.ShapeDtypeStruct((R, D_MODEL), F32), jax.ShapeDtypeStruct(ck.shape, F32),
                 jax.ShapeDtypeStruct(cv.shape, F32), jax.ShapeDtypeStruct(st.shape, F32)]
    scratch = [pltpu.VMEM((R, D_MODEL), F32) for _ in range(4)] + [pltpu.VMEM((R, KV_WIDTH), F32) for _ in range(2)]
    args = [w["sinks2"][layer], h, p, st, ck, cv, tabs[0], tabs[1], tabs[2], bmain, bnew, sel] + _weight_args(w, w_in_bf)
    aliases = {}
    if stacked is not None:
        aliases = {len(args) + k: 1 + k for k in range(len(stacked))}
        args += list(stacked)
        in_specs += [pl.BlockSpec(memory_space=pl.ANY)] * len(stacked)
    return pl.pallas_call(
        functools.partial(_sample_kernel, nb=nb, n_alias=len(aliases)),
        grid=(nb // SB,),
        in_specs=in_specs,
        out_specs=out_specs,
        out_shape=out_shape,
        scratch_shapes=scratch,
        input_output_aliases=aliases,
        compiler_params=pltpu.CompilerParams(dimension_semantics=("arbitrary",), vmem_limit_bytes=VMEM_LIMIT),
        name="sample_layer",
    )(*args)


def _rope_tables(pos):
    pos = np.asarray(pos, np.float64)
    inv_freq = ROPE_THETA ** (-np.arange(0, ROPE_DIM, 2, dtype=np.float64) / ROPE_DIM)
    ang = pos[:, None] * inv_freq[None, :]
    cos, sin = np.cos(ang), np.sin(ang)
    n = pos.shape[0]
    half = ROPE_DIM // 2
    c64 = np.concatenate([cos, cos, np.ones((n, HEAD_DIM - ROPE_DIM))], axis=1)
    a64 = np.concatenate([-sin, np.zeros((n, HEAD_DIM - half))], axis=1)
    b64 = np.concatenate([np.zeros((n, half)), sin, np.zeros((n, HEAD_DIM - ROPE_DIM))], axis=1)
    rep = LANES // HEAD_DIM
    return tuple(jnp.asarray(np.tile(x, (1, rep)), F32) for x in (c64, a64, b64))


def _prompt_bias():
    q = np.arange(WINDOW)[:, None]
    j = np.arange(WINDOW)[None, :]
    prev = np.where(j >= q, 0.0, NEG)
    cur = np.where(j <= q, 0.0, NEG)
    normal = np.concatenate([prev, cur], axis=1)
    first = np.concatenate([np.full_like(prev, NEG), cur], axis=1)
    return jnp.asarray(np.stack([normal, first]), F32)


def _sample_bias(dec):
    rq = np.arange(dec * SB)
    tq, bq = rq // SB, rq % SB
    cm = np.arange(SB * WINDOW)
    bm, jm = cm // WINDOW, cm % WINDOW
    main = np.where((bq[:, None] == bm[None, :]) & (jm[None, :] >= tq[:, None]), 0.0, NEG)
    new = np.where((bq[:, None] == bq[None, :]) & (tq[None, :] <= tq[:, None]), 0.0, NEG)
    sel = np.zeros((SB, dec * SB, WINDOW))
    for r in rq:
        sel[bq[r], r, WINDOW - dec + tq[r]] = 1.0
    return jnp.asarray(main, F32), jnp.asarray(new, F32), jnp.asarray(sel, BF16)


def _prep_weights(norm_g, w_in, q_norm_g, k_norm_g, sinks, pool_map, pool_scale,
                  w_proj_pool, w_proj_attn, w_out, w_ple, w_ple_gate):
    qscale = HEAD_DIM ** -0.5 * LOG2E
    blk = np.kron(np.eye(256 // HEAD_DIM), np.full((HEAD_DIM, HEAD_DIM), 1.0 / HEAD_DIM))
    return dict(
        sinks2=sinks * LOG2E,
        norm_g=norm_g[:, None, :],
        gq=jnp.tile(q_norm_g * qscale, (1, N_HEADS))[:, None, :],
        gk=jnp.tile(k_norm_g, (1, N_KV_HEADS))[:, None, :],
        pool_scale=pool_scale[:, None, :],
        w_in_f32=w_in,
        hn=jnp.asarray(blk, BF16),
        pool_map=pool_map.astype(BF16),
        w_pp=w_proj_pool.astype(BF16), w_pa=w_proj_attn.astype(BF16),
        w_out=w_out.astype(BF16), w_g=w_ple_gate.astype(BF16), w_ple=w_ple.astype(BF16),
    )


def kernel(x_prompt, x_sample, cache_k, cache_v, state_pool, p_prompt, p_sample, norm_g, w_in, q_norm_g, k_norm_g, sinks, pool_map, pool_scale, w_proj_pool, w_proj_attn, w_out, w_ple, w_ple_gate):
    depth = w_in.shape[0]
    B, T, _ = x_prompt.shape
    nb, dec, _ = x_sample.shape
    assert T % TQ == 0 and TQ % WINDOW == 0 and nb % SB == 0
    ptabs = _rope_tables(np.arange(T))
    stabs = _rope_tables(np.repeat(PAST_LEN + np.arange(dec), nb))
    pbias = _prompt_bias()
    bmain, bnew, sel = _sample_bias(dec)

    hp = x_prompt
    hs = x_sample.transpose(1, 0, 2).reshape(dec * nb, D_MODEL)
    ck_all = cache_k.transpose(0, 1, 3, 4, 2).reshape(depth, nb, KV_WIDTH, WINDOW)
    cv_all = cache_v.transpose(0, 1, 3, 4, 2).reshape(depth, nb, KV_WIDTH, WINDOW)
    st_all = state_pool.transpose(0, 2, 1, 3)
    stacked = None
    kp_l, vp_l, pp_l = [], [], []
    w = _prep_weights(norm_g, w_in, q_norm_g, k_norm_g, sinks, pool_map, pool_scale,
                      w_proj_pool, w_proj_attn, w_out, w_ple, w_ple_gate)
    for i in range(depth):
        hp, nk, nv, npool, w_in_bf = _prompt_layer(hp, p_prompt, ptabs, pbias, w, i)
        kp_l.append(nk.reshape(B, WINDOW, N_KV_HEADS, HEAD_DIM))
        vp_l.append(nv.reshape(B, WINDOW, N_KV_HEADS, HEAD_DIM))
        pp_l.append(npool)

        ps_t = p_sample[i].transpose(1, 0, 2).reshape(dec * nb, PLE_DIM)
        hs, *stacked = _sample_layer(hs, ps_t, st_all, ck_all, cv_all, stabs, bmain, bnew, sel, w, w_in_bf, i, stacked)
    uncache = lambda a: a.reshape(depth, nb, N_KV_HEADS, HEAD_DIM, WINDOW).transpose(0, 1, 4, 2, 3)
    new_k_sample, new_v_sample = uncache(stacked[0]), uncache(stacked[1])
    new_pool_sample = stacked[2].transpose(0, 2, 1, 3)
    y_sample = hs.reshape(dec, nb, D_MODEL).transpose(1, 0, 2)
    return (hp, y_sample, jnp.stack(kp_l), jnp.stack(vp_l), jnp.stack(pp_l),
            new_k_sample, new_v_sample, new_pool_sample)
```
